```python
import math
import jax, jax.numpy as jnp
from jax import lax
import numpy as np

D_MODEL = 2048
BATCH = 1
SEQ = 8192
DEPTH = 1

N_DIFF_HEADS = 8
DIFF_HEAD_DIM = D_MODEL // N_DIFF_HEADS // 2
DIFF_V_DIM = 2 * DIFF_HEAD_DIM
QK_WIDTH = 2 * N_DIFF_HEADS * DIFF_HEAD_DIM
ATTN_WIDTH = N_DIFF_HEADS * DIFF_V_DIM
ROPE_THETA = 500000.0
ROPE_DIM = DIFF_HEAD_DIM // 4
Q_BLOCK = 128
CONV_WIDTH = D_MODEL
CONV_K = 3
N_BRANCHES = 2
IN_WIDTH = 2 * QK_WIDTH + ATTN_WIDTH + 3 * CONV_WIDTH + N_BRANCHES * D_MODEL
N_EXPERTS = 32
TOP_K = 4
D_EXPERT = D_MODEL
SWIGLU_LIMIT = 7.0
SWIGLU_ALPHA = 1.702
MOE_BLOCK = 128
NORM_EPS = 1e-5
N_MOD = 6

kernel_name = 'hybrid_diffattn_shortconv_moe_block'


def rms_norm(x, g):
    x32 = x.astype(jnp.float32)
    y = x32 * lax.rsqrt(jnp.mean(x32 * x32, axis=-1, keepdims=True) + NORM_EPS)
    return (y * g.astype(jnp.float32)).astype(x.dtype)


def rope_tables(positions):
    inv_freq = ROPE_THETA ** (-jnp.arange(0, ROPE_DIM, 2, dtype=jnp.float32) / ROPE_DIM)
    ang = positions.astype(jnp.float32)[..., None] * inv_freq
    return jnp.cos(ang), jnp.sin(ang)


def apply_partial_rope(t, cos, sin):
    rot, rest = t[..., :ROPE_DIM], t[..., ROPE_DIM:]
    half = ROPE_DIM // 2
    r1, r2 = rot[..., :half], rot[..., half:]
    cc = cos[:, :, None, :].astype(t.dtype)
    ss = sin[:, :, None, :].astype(t.dtype)
    rot = jnp.concatenate([r1 * cc - r2 * ss, r2 * cc + r1 * ss], axis=-1)
    return jnp.concatenate([rot, rest], axis=-1)


def diff_attention(q, k, v, lam):
    B, S = q.shape[0], q.shape[1]
    n_blocks = S // Q_BLOCK
    scale = DIFF_HEAD_DIM ** -0.5
    k32 = k.astype(jnp.float32)
    v32 = v.astype(jnp.float32)
    qb = q.astype(jnp.float32).reshape(B, n_blocks, Q_BLOCK, 2 * N_DIFF_HEADS, DIFF_HEAD_DIM)
    qb = jnp.moveaxis(qb, 1, 0)
    key_idx = jnp.arange(S)
    neg = jnp.finfo(jnp.float32).min

    def one_block(args):
        q_blk, b_idx = args
        s = jnp.einsum('bqhd,bkhd->bhqk', q_blk, k32) * scale
        q_idx = b_idx * Q_BLOCK + jnp.arange(Q_BLOCK)
        causal = key_idx[None, :] <= q_idx[:, None]
        s = jnp.where(causal[None, None], s, neg)
        p = jax.nn.softmax(s, axis=-1).reshape(B, N_DIFF_HEADS, 2, Q_BLOCK, S)
        a = p[:, :, 0] - lam * p[:, :, 1]
        return jnp.einsum('bhqk,bkhe->bqhe', a, v32)

    o = lax.map(one_block, (qb, jnp.arange(n_blocks)))
    return jnp.moveaxis(o, 0, 1).reshape(B, S, N_DIFF_HEADS, DIFF_V_DIM)


def short_gated_conv(gate_b, gate_c, h_c, conv_w):
    u = gate_c * h_c
    conv = lax.conv_general_dilated(
        u, conv_w[:, None, :].astype(u.dtype), window_strides=(1,),
        padding=[(CONV_K - 1, 0)], dimension_numbers=('NWC', 'WIO', 'NWC'),
        feature_group_count=CONV_WIDTH)
    return gate_b * conv


def hybrid_mixer(h, cos, sin, w_in, conv_w, lq1, lk1, lq2, lk2, subln_g,
                 w_attn_out, w_conv_out, w_o, lambda_init):
    B, S, _ = h.shape
    widths = [QK_WIDTH, QK_WIDTH, ATTN_WIDTH, CONV_WIDTH, CONV_WIDTH, CONV_WIDTH, D_MODEL, D_MODEL]
    split_at = [int(v) for v in np.cumsum(widths)[:-1]]
    proj = h @ w_in
    q, k, v, cb, cc, ch, g_attn, g_conv = jnp.split(proj, split_at, axis=-1)

    q = apply_partial_rope(q.reshape(B, S, 2 * N_DIFF_HEADS, DIFF_HEAD_DIM), cos, sin)
    k = apply_partial_rope(k.reshape(B, S, 2 * N_DIFF_HEADS, DIFF_HEAD_DIM), cos, sin)
    v = v.reshape(B, S, N_DIFF_HEADS, DIFF_V_DIM)
    f32 = jnp.float32
    lam = (jnp.exp(jnp.sum(lq1.astype(f32) * lk1.astype(f32)))
           - jnp.exp(jnp.sum(lq2.astype(f32) * lk2.astype(f32))) + lambda_init)
    o = diff_attention(q, k, v, lam)
    o = rms_norm(o, subln_g) * (1.0 - lambda_init)
    y_attn = o.reshape(B, S, ATTN_WIDTH).astype(h.dtype) @ w_attn_out

    y_conv = short_gated_conv(cb, cc, ch, conv_w) @ w_conv_out

    merged = jax.nn.sigmoid(g_attn) * y_attn + jax.nn.sigmoid(g_conv) * y_conv
    return merged @ w_o


def clamped_swiglu(hup):
    x_glu, x_lin = hup[..., ::2], hup[..., 1::2]
    x_glu = jnp.minimum(x_glu, SWIGLU_LIMIT)
    x_lin = jnp.clip(x_lin, -SWIGLU_LIMIT, SWIGLU_LIMIT)
    return x_glu * jax.nn.sigmoid(SWIGLU_ALPHA * x_glu) * (x_lin + 1.0)


def moe(h, w_router, b_router, w_up, b_up, w_down, b_down):
    B, S, D = h.shape
    T = B * S
    xt = h.reshape(T, D)
    logits = (xt @ w_router).astype(jnp.float32) + b_router.astype(jnp.float32)
    top_vals, top_idx = lax.top_k(logits, TOP_K)
    top_w = jax.nn.softmax(top_vals, axis=-1)

    A = T * TOP_K
    flat_e = top_idx.reshape(A)
    flat_tok = jnp.repeat(jnp.arange(T, dtype=jnp.int32), TOP_K)
    flat_w = top_w.reshape(A)
    order = jnp.argsort(flat_e)
    se, stok, sw = flat_e[order], flat_tok[order], flat_w[order]

    counts = jnp.bincount(flat_e, length=N_EXPERTS)
    start = jnp.cumsum(counts) - counts
    padded = (counts + MOE_BLOCK - 1) // MOE_BLOCK * MOE_BLOCK
    pend = jnp.cumsum(padded)
    pstart = pend - padded
    dest = pstart[se] + (jnp.arange(A) - start[se])

    n_blocks = -(-(A + N_EXPERTS * (MOE_BLOCK - 1)) // MOE_BLOCK)
    P = n_blocks * MOE_BLOCK
    slot_tok = jnp.full((P,), T, dtype=jnp.int32).at[dest].set(stok)
    slot_w = jnp.zeros((P,), jnp.float32).at[dest].set(sw)
    block_e = jnp.minimum(
        jnp.searchsorted(pend, jnp.arange(n_blocks) * MOE_BLOCK, side='right'),
        N_EXPERTS - 1)

    x_pad = jnp.concatenate([xt, jnp.zeros((1, D), xt.dtype)], axis=0)
    xs = x_pad[slot_tok].reshape(n_blocks, MOE_BLOCK, D)

    def expert_block(args):
        xb, e = args
        hup = xb @ w_up[e] + b_up[e]
        return clamped_swiglu(hup) @ w_down[e] + b_down[e]

    ys = lax.map(expert_block, (xs, block_e)).reshape(P, D)
    out = jnp.zeros((T + 1, D), jnp.float32).at[slot_tok].add(
        ys.astype(jnp.float32) * slot_w[:, None])
    return out[:T].reshape(B, S, D).astype(h.dtype)


def setup_inputs(seed: int = 0) -> dict:
    key = jax.random.key(seed)
    ks = jax.random.split(key, 26)
    f32 = jnp.float32
    nrm = lambda k, shape, s: jax.random.normal(k, shape, f32) * s
    L = DEPTH
    return {
        'x': nrm(ks[0], (BATCH, SEQ, D_MODEL), 1.0),
        'c': nrm(ks[1], (BATCH, D_MODEL), 1.0),
        'positions': jnp.broadcast_to(jnp.arange(SEQ, dtype=jnp.int32), (BATCH, SEQ)),
        'w_ada': nrm(ks[2], (L, D_MODEL, N_MOD * D_MODEL), D_MODEL ** -0.5),
        'b_ada': nrm(ks[3], (L, N_MOD * D_MODEL), 0.02),
        'norm1_g': 1.0 + nrm(ks[4], (L, D_MODEL), 0.02),
        'w_in': nrm(ks[5], (L, D_MODEL, IN_WIDTH), D_MODEL ** -0.5),
        'conv_w': nrm(ks[6], (L, CONV_K, CONV_WIDTH), CONV_K ** -0.5),
        'lambda_q1': nrm(ks[7], (L, DIFF_HEAD_DIM), 0.1),
        'lambda_k1': nrm(ks[8], (L, DIFF_HEAD_DIM), 0.1),
        'lambda_q2': nrm(ks[9], (L, DIFF_HEAD_DIM), 0.1),
        'lambda_k2': nrm(ks[10], (L, DIFF_HEAD_DIM), 0.1),
        'subln_g': 1.0 + nrm(ks[11], (L, DIFF_V_DIM), 0.02),
        'w_attn_out': nrm(ks[12], (L, ATTN_WIDTH, D_MODEL), ATTN_WIDTH ** -0.5),
        'w_conv_out': nrm(ks[13], (L, CONV_WIDTH, D_MODEL), CONV_WIDTH ** -0.5),
        'w_o': nrm(ks[14], (L, D_MODEL, D_MODEL), D_MODEL ** -0.5),
        'norm2_g': 1.0 + nrm(ks[15], (L, D_MODEL), 0.02),
        'w_router': nrm(ks[16], (L, D_MODEL, N_EXPERTS), D_MODEL ** -0.5),
        'b_router': nrm(ks[17], (L, N_EXPERTS), 0.01),
        'w_up': nrm(ks[18], (L, N_EXPERTS, D_MODEL, 2 * D_EXPERT), D_MODEL ** -0.5),
        'b_up': nrm(ks[19], (L, N_EXPERTS, 2 * D_EXPERT), 0.02),
        'w_down': nrm(ks[20], (L, N_EXPERTS, D_EXPERT, D_MODEL), D_EXPERT ** -0.5),
        'b_down': nrm(ks[21], (L, N_EXPERTS, D_MODEL), 0.02),
        'final_g': 1.0 + nrm(ks[22], (D_MODEL,), 0.02),
    }


def reference(x, c, positions, w_ada, b_ada, norm1_g, w_in, conv_w, lambda_q1, lambda_k1,
              lambda_q2, lambda_k2, subln_g, w_attn_out, w_conv_out, w_o, norm2_g,
              w_router, b_router, w_up, b_up, w_down, b_down, final_g):
    cos, sin = rope_tables(positions)
    c_act = jax.nn.silu(c)
    for l in range(DEPTH):
        lambda_init = 0.8 - 0.6 * math.exp(-0.3 * l)
        mod = c_act @ w_ada[l] + b_ada[l]
        sh1, sc1, g1, sh2, sc2, g2 = [m[:, None, :] for m in jnp.split(mod, N_MOD, axis=-1)]
        h = rms_norm(x, norm1_g[l]) * (1.0 + sc1) + sh1
        x = x + g1 * hybrid_mixer(h, cos, sin, w_in[l], conv_w[l], lambda_q1[l], lambda_k1[l],
                                  lambda_q2[l], lambda_k2[l], subln_g[l], w_attn_out[l],
                                  w_conv_out[l], w_o[l], lambda_init)
        h = rms_norm(x, norm2_g[l]) * (1.0 + sc2) + sh2
        x = x + g2 * moe(h, w_router[l], b_router[l], w_up[l], b_up[l], w_down[l], b_down[l])
    return rms_norm(x, final_g)
```

```python
import functools
import math

import numpy as np
import jax
import jax.numpy as jnp
from jax import lax
from jax.experimental import pallas as pl
from jax.experimental.pallas import tpu as pltpu

F32 = jnp.float32
BF16 = jnp.bfloat16

D_MODEL = 2048
N_HEADS = 8
HEAD_DIM = 128
V_DIM = 2 * HEAD_DIM
ROPE_DIM = 32
ROPE_THETA = 500000.0
IN_WIDTH = 8 * D_MODEL
N_EXPERTS = 32
TOP_K = 4
D_EXPERT = D_MODEL
SWIGLU_LIMIT = 7.0
SWIGLU_ALPHA = 1.702
NORM_EPS = 1e-5
N_MOD = 6
LAMBDA_INIT = 0.8 - 0.6 * math.exp(-0.3 * 0)

VMEM_LIMIT_BYTES = 56 * 1024 * 1024
NEG_BIG = -1e30


def _params(*sem):
    return pltpu.CompilerParams(dimension_semantics=sem, vmem_limit_bytes=VMEM_LIMIT_BYTES)


def _mod_kernel(c_ref, w_ref, b_ref, o_ref):
    c = c_ref[...]
    c_act = c * jax.nn.sigmoid(c)
    o_ref[...] = jnp.sum(w_ref[...] * c_act, axis=0, keepdims=True) + b_ref[...]


def _modulation(c, w_ada, b_ada):
    n = w_ada.shape[1]
    tn = 1024
    return pl.pallas_call(
        _mod_kernel,
        grid=(n // tn,),
        in_specs=[pl.BlockSpec((D_MODEL, 1), lambda j: (0, 0)),
                  pl.BlockSpec((D_MODEL, tn), lambda j: (0, j)),
                  pl.BlockSpec((1, tn), lambda j: (0, j))],
        out_specs=pl.BlockSpec((1, tn), lambda j: (0, j)),
        out_shape=jax.ShapeDtypeStruct((1, n), F32),
        compiler_params=_params("arbitrary"),
        name="adaln_mod",
    )(c.reshape(D_MODEL, 1), w_ada, b_ada.reshape(1, n))


def _inproj_kernel(x_ref, g_ref, sc_ref, sh_ref, pos_ref, invf_ref, w_ref, o_ref,
                   h_scr, cos_scr, sa_scr, sb_scr, *, n_rope_tiles, tn):
    j = pl.program_id(1)

    @pl.when(j == 0)
    def _():
        x = x_ref[...]
        ms = jnp.mean(x * x, axis=-1, keepdims=True)
        y = x * lax.rsqrt(ms + NORM_EPS) * g_ref[...]
        h_scr[...] = (y * (1.0 + sc_ref[...]) + sh_ref[...]).astype(BF16)
        ang = pos_ref[...].astype(F32) * invf_ref[...]
        cs = jnp.cos(ang)
        sn = jnp.sin(ang)
        lane = lax.broadcasted_iota(jnp.int32, ang.shape, 1)
        cos_scr[...] = cs
        sa_scr[...] = jnp.where(lane >= ROPE_DIM // 2, sn, 0.0)
        sb_scr[...] = jnp.where(lane < ROPE_DIM // 2, -sn, 0.0)

    acc = jnp.dot(h_scr[...], w_ref[...].astype(BF16), preferred_element_type=F32)

    @pl.when(j < n_rope_tiles)
    def _():
        for hh in range(tn // HEAD_DIM):
            t = acc[:, hh * HEAD_DIM:(hh + 1) * HEAD_DIM]
            r = (t * cos_scr[...]
                 + pltpu.roll(t, ROPE_DIM // 2, 1) * sa_scr[...]
                 + pltpu.roll(t, HEAD_DIM - ROPE_DIM // 2, 1) * sb_scr[...])
            o_ref[:, hh * HEAD_DIM:(hh + 1) * HEAD_DIM] = r.astype(BF16)

    @pl.when(j >= n_rope_tiles)
    def _():
        o_ref[...] = acc.astype(BF16)


def _in_projection(x2d, g, sc, sh, pos_col, invf, w_in):
    s = x2d.shape[0]
    tm = min(1024, s)
    tn = 512
    kern = functools.partial(_inproj_kernel, n_rope_tiles=(2 * D_MODEL) // tn, tn=tn)
    row = lambda i, j: (0, 0)
    return pl.pallas_call(
        kern,
        grid=(s // tm, IN_WIDTH // tn),
        in_specs=[pl.BlockSpec((tm, D_MODEL), lambda i, j: (i, 0)),
                  pl.BlockSpec((1, D_MODEL), row),
                  pl.BlockSpec((1, D_MODEL), row),
                  pl.BlockSpec((1, D_MODEL), row),
                  pl.BlockSpec((tm, 1), lambda i, j: (i, 0)),
                  pl.BlockSpec((1, HEAD_DIM), row),
                  pl.BlockSpec((D_MODEL, tn), lambda i, j: (0, j))],
        out_specs=pl.BlockSpec((tm, tn), lambda i, j: (i, j)),
        out_shape=jax.ShapeDtypeStruct((s, IN_WIDTH), BF16),
        scratch_shapes=[pltpu.VMEM((tm, D_MODEL), BF16),
                        pltpu.VMEM((tm, HEAD_DIM), F32),
                        pltpu.VMEM((tm, HEAD_DIM), F32),
                        pltpu.VMEM((tm, HEAD_DIM), F32)],
        compiler_params=_params("arbitrary", "arbitrary"),
        name="in_projection",
    )(x2d, g, sc, sh, pos_col, invf, w_in)


def _attn_kernel(lam_ref, q_ref, k_ref, v_ref, g_ref, o_ref, m_scr, l_scr, acc_scr, *, tq):
    i = pl.program_id(1)
    scale = HEAD_DIM ** -0.5
    m_scr[...] = jnp.full(m_scr.shape, NEG_BIG, F32)
    l_scr[...] = jnp.zeros(l_scr.shape, F32)
    acc_scr[...] = jnp.zeros(acc_scr.shape, F32)
    q = q_ref[...]

    def chunk(j, masked):
        start = pl.multiple_of(j * tq, tq)
        k = k_ref[pl.ds(start, tq), :]
        v = v_ref[pl.ds(start, tq), :]
        for c in range(2):
            s = lax.dot_general(q[:, c * HEAD_DIM:(c + 1) * HEAD_DIM],
                                k[:, c * HEAD_DIM:(c + 1) * HEAD_DIM],
                                (((1,), (1,)), ((), ())), preferred_element_type=F32) * scale
            if masked:
                row = lax.broadcasted_iota(jnp.int32, s.shape, 0)
                col = lax.broadcasted_iota(jnp.int32, s.shape, 1)
                s = jnp.where(col > row, NEG_BIG, s)
            m_old = m_scr[c]
            m_new = jnp.maximum(m_old, jnp.max(s, axis=-1, keepdims=True))
            alpha = jnp.exp(m_old - m_new)
            p = jnp.exp(s - m_new)
            l_scr[c] = alpha * l_scr[c] + jnp.sum(p, axis=-1, keepdims=True)
            acc_scr[c] = alpha * acc_scr[c] + jnp.dot(p.astype(BF16), v, preferred_element_type=F32)
            m_scr[c] = m_new

    def body(j, carry):
        chunk(j, False)
        return carry

    lax.fori_loop(0, i, body, 0)
    chunk(i, True)

    lam = lam_ref[0, 0]
    o = acc_scr[0] / l_scr[0] - lam * (acc_scr[1] / l_scr[1])
    ms = jnp.mean(o * o, axis=-1, keepdims=True)
    o = o * lax.rsqrt(ms + NORM_EPS) * g_ref[...] * (1.0 - LAMBDA_INIT)
    o_ref[...] = o.astype(BF16)


def _diff_attention(proj, lam, subln_g):
    s = proj.shape[0]
    tq = min(512, s)
    kern = functools.partial(_attn_kernel, tq=tq)
    kblk = (2 * D_MODEL) // V_DIM // 2
    return pl.pallas_call(
        kern,
        grid=(N_HEADS, s // tq),
        in_specs=[pl.BlockSpec(memory_space=pltpu.SMEM),
                  pl.BlockSpec((tq, V_DIM), lambda h, i: (i, h)),
                  pl.BlockSpec((s, V_DIM), lambda h, i: (0, kblk + h)),
                  pl.BlockSpec((s, V_DIM), lambda h, i: (0, 2 * kblk + h)),
                  pl.BlockSpec((1, V_DIM), lambda h, i: (0, 0))],
        out_specs=pl.BlockSpec((tq, V_DIM), lambda h, i: (i, h)),
        out_shape=jax.ShapeDtypeStruct((s, N_HEADS * V_DIM), BF16),
        scratch_shapes=[pltpu.VMEM((2, tq, 1), F32),
                        pltpu.VMEM((2, tq, 1), F32),
                        pltpu.VMEM((2, tq, V_DIM), F32)],
        compiler_params=_params("arbitrary", "arbitrary"),
        name="diff_attention",
    )(lam, proj, proj, proj, subln_g)


HALO = 16


def _merge_kernel(o_ref, cb_ref, cc_ref, ch_ref, hcc_ref, hch_ref, ga_ref, gc_ref, cw_ref,
                  wa_ref, wc_ref, out_ref, u_scr, pad_scr, *, tm):
    i = pl.program_id(0)
    j = pl.program_id(1)

    @pl.when(j == 0)
    def _():
        halo = hcc_ref[...].astype(F32) * hch_ref[...].astype(F32)
        pad_scr[0:HALO, :] = jnp.where(i == 0, 0.0, halo)
        up = cc_ref[...].astype(F32) * ch_ref[...].astype(F32)
        pad_scr[HALO:HALO + tm, :] = up
        cw = cw_ref[...]
        conv = (cw[0:1, :] * pad_scr[HALO - 2:HALO - 2 + tm, :]
                + cw[1:2, :] * pad_scr[HALO - 1:HALO - 1 + tm, :]
                + cw[2:3, :] * up)
        u_scr[...] = (cb_ref[...].astype(F32) * conv).astype(BF16)

    ya = jnp.dot(o_ref[...], wa_ref[...].astype(BF16), preferred_element_type=F32)
    yc = jnp.dot(u_scr[...], wc_ref[...].astype(BF16), preferred_element_type=F32)
    merged = (jax.nn.sigmoid(ga_ref[...].astype(F32)) * ya
              + jax.nn.sigmoid(gc_ref[...].astype(F32)) * yc)
    out_ref[...] = merged.astype(BF16)


def _branch_merge(o_attn, proj, conv_w, w_attn_out, w_conv_out):
    s = proj.shape[0]
    tm = min(512, s)
    tn = 512
    nb = D_MODEL // tn
    hb = tm // HALO
    kern = functools.partial(_merge_kernel, tm=tm)
    halo_idx = lambda col: (lambda i, j: (jnp.maximum(i * hb - 1, 0), col))
    return pl.pallas_call(
        kern,
        grid=(s // tm, nb),
        in_specs=[pl.BlockSpec((tm, D_MODEL), lambda i, j: (i, 0)),
                  pl.BlockSpec((tm, D_MODEL), lambda i, j: (i, 3)),
                  pl.BlockSpec((tm, D_MODEL), lambda i, j: (i, 4)),
                  pl.BlockSpec((tm, D_MODEL), lambda i, j: (i, 5)),
                  pl.BlockSpec((HALO, D_MODEL), halo_idx(4)),
                  pl.BlockSpec((HALO, D_MODEL), halo_idx(5)),
                  pl.BlockSpec((tm, tn), lambda i, j: (i, 6 * nb + j)),
                  pl.BlockSpec((tm, tn), lambda i, j: (i, 7 * nb + j)),
                  pl.BlockSpec((3, D_MODEL), lambda i, j: (0, 0)),
                  pl.BlockSpec((D_MODEL, tn), lambda i, j: (0, j)),
                  pl.BlockSpec((D_MODEL, tn), lambda i, j: (0, j))],
        out_specs=pl.BlockSpec((tm, tn), lambda i, j: (i, j)),
        out_shape=jax.ShapeDtypeStruct((s, D_MODEL), BF16),
        scratch_shapes=[pltpu.VMEM((tm, D_MODEL), BF16),
                        pltpu.VMEM((tm + HALO, D_MODEL), F32)],
        compiler_params=_params("arbitrary", "arbitrary"),
        name="branch_merge",
    )(o_attn, proj, proj, proj, proj, proj, proj, proj, conv_w, w_attn_out, w_conv_out)


def _oproj_kernel(m_ref, x_ref, g1_ref, w_ref, n2_ref, sc_ref, sh_ref, wr_ref, br_ref,
                  x1_ref, h2_ref, lg_ref, *, tn):
    j = pl.program_id(1)
    nj = pl.num_programs(1)
    col = pl.multiple_of(j * tn, tn)
    y = jnp.dot(m_ref[...], w_ref[...].astype(BF16), preferred_element_type=F32)
    x1_ref[:, pl.ds(col, tn)] = x_ref[:, pl.ds(col, tn)] + g1_ref[:, pl.ds(col, tn)] * y

    @pl.when(j == nj - 1)
    def _():
        x1 = x1_ref[...]
        ms = jnp.mean(x1 * x1, axis=-1, keepdims=True)
        h2 = x1 * lax.rsqrt(ms + NORM_EPS) * n2_ref[...] * (1.0 + sc_ref[...]) + sh_ref[...]
        h2_ref[...] = h2
        lg_ref[...] = lax.dot_general(wr_ref[...], h2, (((1,), (1,)), ((), ())),
                                      precision=lax.Precision.HIGHEST,
                                      preferred_element_type=F32) + br_ref[...]


def _out_projection(merged, x2d, g1, w_o, n2, sc2, sh2, w_router_t, b_router_col):
    s = x2d.shape[0]
    tm = min(512, s)
    tn = 512
    kern = functools.partial(_oproj_kernel, tn=tn)
    row = lambda i, j: (0, 0)
    return pl.pallas_call(
        kern,
        grid=(s // tm, D_MODEL // tn),
        in_specs=[pl.BlockSpec((tm, D_MODEL), lambda i, j: (i, 0)),
                  pl.BlockSpec((tm, D_MODEL), lambda i, j: (i, 0)),
                  pl.BlockSpec((1, D_MODEL), row),
                  pl.BlockSpec((D_MODEL, tn), lambda i, j: (0, j)),
                  pl.BlockSpec((1, D_MODEL), row),
                  pl.BlockSpec((1, D_MODEL), row),
                  pl.BlockSpec((1, D_MODEL), row),
                  pl.BlockSpec((N_EXPERTS, D_MODEL), row),
                  pl.BlockSpec((N_EXPERTS, 1), row)],
        out_specs=[pl.BlockSpec((tm, D_MODEL), lambda i, j: (i, 0)),
                   pl.BlockSpec((tm, D_MODEL), lambda i, j: (i, 0)),
                   pl.BlockSpec((N_EXPERTS, tm), lambda i, j: (0, i))],
        out_shape=[jax.ShapeDtypeStruct((s, D_MODEL), F32),
                   jax.ShapeDtypeStruct((s, D_MODEL), F32),
                   jax.ShapeDtypeStruct((N_EXPERTS, s), F32)],
        compiler_params=_params("arbitrary", "arbitrary"),
        name="out_projection",
    )(merged, x2d, g1, w_o, n2, sc2, sh2, w_router_t, b_router_col)


ROUTE_CHUNK = 256


def _route_kernel(lg_ref, dest_ref, wts_ref, cnt_ref, ind_scr, *, t, blk):
    lg = lg_ref[...]
    e_iota = lax.broadcasted_iota(jnp.int32, lg.shape, 0)
    sels, vals = [], []
    for _ in range(TOP_K):
        m = jnp.max(lg, axis=0, keepdims=True)
        idx = jnp.min(jnp.where(lg == m, e_iota, N_EXPERTS), axis=0, keepdims=True)
        sel = e_iota == idx
        sels.append(sel)
        vals.append(m)
        lg = jnp.where(sel, -jnp.inf, lg)
    exps = [jnp.exp(v - vals[0]) for v in vals]
    denom = exps[0] + exps[1] + exps[2] + exps[3]
    for r in range(TOP_K):
        wts_ref[r:r + 1, :] = exps[r] / denom

    ind = jnp.zeros(lg.shape, F32)
    for sel in sels:
        ind = ind + jnp.where(sel, 1.0, 0.0)
    ind_scr[...] = ind
    counts = jnp.sum(ind, axis=1, keepdims=True)
    padded = jnp.ceil(counts / blk) * blk
    er = lax.broadcasted_iota(jnp.int32, (N_EXPERTS, N_EXPERTS), 0)
    ec = lax.broadcasted_iota(jnp.int32, (N_EXPERTS, N_EXPERTS), 1)
    lower = jnp.where(ec < er, 1.0, 0.0)
    pstart = jnp.dot(lower, jnp.broadcast_to(padded, (N_EXPERTS, 128)),
                     precision=lax.Precision.HIGHEST, preferred_element_type=F32)[:, 0:1]
    cnt_ref[...] = jnp.concatenate([counts, pstart], axis=1).astype(jnp.int32)

    ur = lax.broadcasted_iota(jnp.int32, (ROUTE_CHUNK, ROUTE_CHUNK), 0)
    uc = lax.broadcasted_iota(jnp.int32, (ROUTE_CHUNK, ROUTE_CHUNK), 1)
    upper = jnp.where(ur < uc, 1.0, 0.0).astype(BF16)
    carry = pstart
    for c in range(t // ROUTE_CHUNK):
        sl = slice(c * ROUTE_CHUNK, (c + 1) * ROUTE_CHUNK)
        ind_c = ind_scr[:, sl]
        pre = jnp.dot(ind_c.astype(BF16), upper, preferred_element_type=F32) + carry
        for r in range(TOP_K):
            d = jnp.sum(jnp.where(sels[r][:, sl], pre, 0.0), axis=0, keepdims=True)
            dest_ref[r:r + 1, sl] = d.astype(jnp.int32)
        carry = carry + jnp.sum(ind_c, axis=1, keepdims=True)


def _route(logits_t, blk):
    t = logits_t.shape[1]
    kern = functools.partial(_route_kernel, t=t, blk=blk)
    return pl.pallas_call(
        kern,
        out_shape=[jax.ShapeDtypeStruct((TOP_K, t), jnp.int32),
                   jax.ShapeDtypeStruct((TOP_K, t), F32),
                   jax.ShapeDtypeStruct((N_EXPERTS, 2), jnp.int32)],
        scratch_shapes=[pltpu.VMEM((N_EXPERTS, t), F32)],
        compiler_params=pltpu.CompilerParams(vmem_limit_bytes=VMEM_LIMIT_BYTES),
        name="route",
    )(logits_t)


def _expert_kernel(be_ref, nu_ref, x_ref, wg_ref, wl_ref, bg_ref, bl_ref, wd_ref, bd_ref, sw_ref,
                   y_ref, acc_scr):
    b = pl.program_id(0)
    f = pl.program_id(1)
    nf = pl.num_programs(1)

    @pl.when(b < nu_ref[0])
    def _():
        x = x_ref[...]
        glu = jnp.dot(x, wg_ref[0], preferred_element_type=F32) + bg_ref[0]
        lin = jnp.dot(x, wl_ref[0], preferred_element_type=F32) + bl_ref[0]
        glu = jnp.minimum(glu, SWIGLU_LIMIT)
        lin = jnp.clip(lin, -SWIGLU_LIMIT, SWIGLU_LIMIT)
        act = glu * jax.nn.sigmoid(SWIGLU_ALPHA * glu) * (lin + 1.0)
        part = jnp.dot(act.astype(BF16), wd_ref[0], preferred_element_type=F32)

        @pl.when(f == 0)
        def _():
            acc_scr[...] = part

        @pl.when(f > 0)
        def _():
            acc_scr[...] += part

        @pl.when(f == nf - 1)
        def _():
            y_ref[...] = (acc_scr[...] + bd_ref[0]) * sw_ref[...]


def _experts(xs, block_e, n_used, wg, wl, bg, bl, wd, bd, slot_w, blk):
    p = xs.shape[0]
    nb = p // blk
    tf = 512
    nf = D_EXPERT // tf

    def blk_idx(b, nu):
        return jnp.minimum(b, nu[0] - 1)

    def f_idx(b, f, nu):
        return jnp.where(b < nu[0], f, nf - 1)

    grid_spec = pltpu.PrefetchScalarGridSpec(
        num_scalar_prefetch=2,
        grid=(nb, nf),
        in_specs=[pl.BlockSpec((blk, D_MODEL), lambda b, f, be, nu: (blk_idx(b, nu), 0)),
                  pl.BlockSpec((1, D_MODEL, tf), lambda b, f, be, nu: (be[blk_idx(b, nu)], 0, f_idx(b, f, nu))),
                  pl.BlockSpec((1, D_MODEL, tf), lambda b, f, be, nu: (be[blk_idx(b, nu)], 0, f_idx(b, f, nu))),
                  pl.BlockSpec((1, 1, tf), lambda b, f, be, nu: (be[blk_idx(b, nu)], 0, f_idx(b, f, nu))),
                  pl.BlockSpec((1, 1, tf), lambda b, f, be, nu: (be[blk_idx(b, nu)], 0, f_idx(b, f, nu))),
                  pl.BlockSpec((1, tf, D_MODEL), lambda b, f, be, nu: (be[blk_idx(b, nu)], f_idx(b, f, nu), 0)),
                  pl.BlockSpec((1, 1, D_MODEL), lambda b, f, be, nu: (be[blk_idx(b, nu)], 0, 0)),
                  pl.BlockSpec((blk, 1), lambda b, f, be, nu: (blk_idx(b, nu), 0))],
        out_specs=pl.BlockSpec((blk, D_MODEL), lambda b, f, be, nu: (blk_idx(b, nu), 0)),
        scratch_shapes=[pltpu.VMEM((blk, D_MODEL), F32)],
    )
    return pl.pallas_call(
        _expert_kernel,
        grid_spec=grid_spec,
        out_shape=jax.ShapeDtypeStruct((p, D_MODEL), F32),
        compiler_params=_params("arbitrary", "arbitrary"),
        name="experts",
    )(block_e, n_used, xs, wg, wl, bg, bl, wd, bd, slot_w)


def _final_kernel(x1_ref, y_ref, g2_ref, fg_ref, o_ref):
    moe = y_ref[0] + y_ref[1] + y_ref[2] + y_ref[3]
    x2 = x1_ref[...] + g2_ref[...] * moe
    ms = jnp.mean(x2 * x2, axis=-1, keepdims=True)
    o_ref[...] = x2 * lax.rsqrt(ms + NORM_EPS) * fg_ref[...]


def _final(x1, yk, g2, final_g):
    s = x1.shape[0]
    tm = min(256, s)
    return pl.pallas_call(
        _final_kernel,
        grid=(s // tm,),
        in_specs=[pl.BlockSpec((tm, D_MODEL), lambda i: (i, 0)),
                  pl.BlockSpec((TOP_K, tm, D_MODEL), lambda i: (0, i, 0)),
                  pl.BlockSpec((1, D_MODEL), lambda i: (0, 0)),
                  pl.BlockSpec((1, D_MODEL), lambda i: (0, 0))],
        out_specs=pl.BlockSpec((tm, D_MODEL), lambda i: (i, 0)),
        out_shape=jax.ShapeDtypeStruct((s, D_MODEL), F32),
        compiler_params=_params("arbitrary"),
        name="final_norm",
    )(x1, yk, g2, final_g)


def _lam_kernel(q1_ref, k1_ref, q2_ref, k2_ref, o_ref):
    a = jnp.sum(q1_ref[...] * k1_ref[...], axis=-1, keepdims=True)
    b = jnp.sum(q2_ref[...] * k2_ref[...], axis=-1, keepdims=True)
    o_ref[...] = jnp.exp(a) - jnp.exp(b) + LAMBDA_INIT


def _lambda(lq1, lk1, lq2, lk2):
    return pl.pallas_call(
        _lam_kernel,
        out_shape=jax.ShapeDtypeStruct((1, 1), F32),
        name="diff_lambda",
    )(lq1, lk1, lq2, lk2)


MOE_BLOCK = 512


def kernel(x, c, positions, w_ada, b_ada, norm1_g, w_in, conv_w, lambda_q1, lambda_k1, lambda_q2, lambda_k2, subln_g, w_attn_out, w_conv_out, w_o, norm2_g, w_router, b_router, w_up, b_up, w_down, b_down, final_g):
    b, s, d = x.shape
    assert b == 1 and d == D_MODEL
    t = s
    x2d = x.reshape(t, d)

    mod = _modulation(c, w_ada[0], b_ada[0])
    sh1, sc1, g1, sh2, sc2, g2 = [mod[:, k * d:(k + 1) * d] for k in range(N_MOD)]

    inv_freq = ROPE_THETA ** (-jnp.arange(0, ROPE_DIM, 2, dtype=F32) / ROPE_DIM)
    invf = jnp.concatenate([inv_freq, inv_freq, jnp.zeros((HEAD_DIM - ROPE_DIM,), F32)]).reshape(1, HEAD_DIM)
    pos_col = positions.reshape(t, 1)

    proj = _in_projection(x2d, norm1_g, sc1, sh1, pos_col, invf, w_in[0])
    lam = _lambda(lambda_q1, lambda_k1, lambda_q2, lambda_k2)
    o_attn = _diff_attention(proj, lam, subln_g)
    merged = _branch_merge(o_attn, proj, conv_w[0], w_attn_out[0], w_conv_out[0])
    x1, h2, logits_t = _out_projection(merged, x2d, g1, w_o[0], norm2_g, sc2, sh2,
                                       w_router[0].T, b_router[0].reshape(N_EXPERTS, 1))

    blk = MOE_BLOCK
    dest, wts, cnt = _route(logits_t, blk)
    a = t * TOP_K
    nb = -(-(a + N_EXPERTS * (blk - 1)) // blk)
    p = nb * blk
    counts, pstart = cnt[:, 0], cnt[:, 1]
    pend = pstart + (counts + blk - 1) // blk * blk
    n_used = (pend[-1] // blk).reshape(1).astype(jnp.int32)
    block_e = jnp.minimum(jnp.searchsorted(pend, jnp.arange(nb, dtype=jnp.int32) * blk, side='right'),
                          N_EXPERTS - 1).astype(jnp.int32)
    tok = jnp.broadcast_to(jnp.arange(t, dtype=jnp.int32)[None, :], (TOP_K, t))
    slot_tok = jnp.full((p,), t, jnp.int32).at[dest.reshape(-1)].set(tok.reshape(-1))
    slot_w = jnp.zeros((p,), F32).at[dest.reshape(-1)].set(wts.reshape(-1))

    h2_pad = jnp.concatenate([h2.astype(BF16), jnp.zeros((1, d), BF16)], axis=0)
    xs = h2_pad[slot_tok]

    wg = w_up[0, :, :, 0::2].astype(BF16)
    wl = w_up[0, :, :, 1::2].astype(BF16)
    bg = b_up[0, :, 0::2].reshape(N_EXPERTS, 1, D_EXPERT)
    bl = b_up[0, :, 1::2].reshape(N_EXPERTS, 1, D_EXPERT)
    wd = w_down[0].astype(BF16)
    bd = b_down[0].reshape(N_EXPERTS, 1, d)
    ys = _experts(xs, block_e, n_used, wg, wl, bg, bl, wd, bd, slot_w.reshape(p, 1), blk)

    yk = ys[dest]
    out = _final(x1, yk, g2, final_g.reshape(1, d))
    return out.reshape(b, s, d)
```

```python
import functools
import math

import numpy as np
import jax
import jax.numpy as jnp
from jax import lax
from jax.experimental import pallas as pl
from jax.experimental.pallas import tpu as pltpu

F32 = jnp.float32
BF16 = jnp.bfloat16

D_MODEL = 2048
N_HEADS = 8
HEAD_DIM = 128
V_DIM = 2 * HEAD_DIM
ROPE_DIM = 32
ROPE_THETA = 500000.0
IN_WIDTH = 8 * D_MODEL
N_EXPERTS = 32
TOP_K = 4
D_EXPERT = D_MODEL
SWIGLU_LIMIT = 7.0
SWIGLU_ALPHA = 1.702
NORM_EPS = 1e-5
N_MOD = 6
LAMBDA_INIT = 0.8 - 0.6 * math.exp(-0.3 * 0)

VMEM_LIMIT_BYTES = 56 * 1024 * 1024
NEG_BIG = -1e30
LANES = 128
Q_PRESCALE = HEAD_DIM ** -0.5 * math.log2(math.e)


def _params(*sem):
    return pltpu.CompilerParams(dimension_semantics=sem, vmem_limit_bytes=VMEM_LIMIT_BYTES)


def _mod_kernel(c_ref, w_ref, b_ref, o_ref):
    c = c_ref[...]
    c_act = c * jax.nn.sigmoid(c)
    o_ref[...] = jnp.sum(w_ref[...] * c_act, axis=0, keepdims=True) + b_ref[...]


def _modulation(c, w_ada, b_ada):
    n = w_ada.shape[1]
    tn = 1024
    return pl.pallas_call(
        _mod_kernel,
        grid=(n // tn,),
        in_specs=[pl.BlockSpec((D_MODEL, 1), lambda j: (0, 0)),
                  pl.BlockSpec((D_MODEL, tn), lambda j: (0, j)),
                  pl.BlockSpec((1, tn), lambda j: (0, j))],
        out_specs=pl.BlockSpec((1, tn), lambda j: (0, j)),
        out_shape=jax.ShapeDtypeStruct((1, n), F32),
        compiler_params=_params("arbitrary"),
        name="adaln_mod",
    )(c.reshape(D_MODEL, 1), w_ada, b_ada.reshape(1, n))


def _inproj_kernel(x_ref, g_ref, sc_ref, sh_ref, pos_ref, invf_ref, w_ref, o_ref,
                   h_scr, cos_scr, sa_scr, sb_scr, *, n_rope_tiles, tn):
    j = pl.program_id(1)

    @pl.when(j == 0)
    def _():
        x = x_ref[...]
        ms = jnp.mean(x * x, axis=-1, keepdims=True)
        y = x * lax.rsqrt(ms + NORM_EPS) * g_ref[...]
        h_scr[...] = (y * (1.0 + sc_ref[...]) + sh_ref[...]).astype(BF16)
        ang = pos_ref[...].astype(F32) * invf_ref[...]
        cs = jnp.cos(ang)
        sn = jnp.sin(ang)
        lane = lax.broadcasted_iota(jnp.int32, ang.shape, 1)
        cos_scr[...] = cs
        sa_scr[...] = jnp.where(lane >= ROPE_DIM // 2, sn, 0.0)
        sb_scr[...] = jnp.where(lane < ROPE_DIM // 2, -sn, 0.0)

    acc = jnp.dot(h_scr[...], w_ref[...].astype(BF16), preferred_element_type=F32)

    @pl.when(j < n_rope_tiles)
    def _():
        fac = jnp.where(j < n_rope_tiles // 2, Q_PRESCALE, 1.0)
        for hh in range(tn // HEAD_DIM):
            t = acc[:, hh * HEAD_DIM:(hh + 1) * HEAD_DIM]
            r = (t * cos_scr[...]
                 + pltpu.roll(t, ROPE_DIM // 2, 1) * sa_scr[...]
                 + pltpu.roll(t, HEAD_DIM - ROPE_DIM // 2, 1) * sb_scr[...])
            o_ref[:, hh * HEAD_DIM:(hh + 1) * HEAD_DIM] = (r * fac).astype(BF16)

    @pl.when(j >= n_rope_tiles)
    def _():
        o_ref[...] = acc.astype(BF16)


def _in_projection(x2d, g, sc, sh, pos_col, invf, w_in):
    s = x2d.shape[0]
    tm = min(1024, s)
    tn = 512
    kern = functools.partial(_inproj_kernel, n_rope_tiles=(2 * D_MODEL) // tn, tn=tn)
    row = lambda i, j: (0, 0)
    return pl.pallas_call(
        kern,
        grid=(s // tm, IN_WIDTH // tn),
        in_specs=[pl.BlockSpec((tm, D_MODEL), lambda i, j: (i, 0)),
                  pl.BlockSpec((1, D_MODEL), row),
                  pl.BlockSpec((1, D_MODEL), row),
                  pl.BlockSpec((1, D_MODEL), row),
                  pl.BlockSpec((tm, 1), lambda i, j: (i, 0)),
                  pl.BlockSpec((1, HEAD_DIM), row),
                  pl.BlockSpec((D_MODEL, tn), lambda i, j: (0, j))],
        out_specs=pl.BlockSpec((tm, tn), lambda i, j: (i, j)),
        out_shape=jax.ShapeDtypeStruct((s, IN_WIDTH), BF16),
        scratch_shapes=[pltpu.VMEM((tm, D_MODEL), BF16),
                        pltpu.VMEM((tm, HEAD_DIM), F32),
                        pltpu.VMEM((tm, HEAD_DIM), F32),
                        pltpu.VMEM((tm, HEAD_DIM), F32)],
        compiler_params=_params("arbitrary", "arbitrary"),
        name="in_projection",
    )(x2d, g, sc, sh, pos_col, invf, w_in)


ATTN_TQ = 512
ATTN_TK = 512
_NT_DIMS = (((1,), (1,)), ((), ()))


def _attn_kernel(lam_ref, q_ref, k_ref, v_ref, g_ref, o_ref, m_scr, l_scr, acc_scr, p_scr, *, tq, tk):
    i = pl.program_id(1)
    m_scr[...] = jnp.full(m_scr.shape, NEG_BIG, F32)
    l_scr[...] = jnp.zeros(l_scr.shape, F32)
    acc_scr[...] = jnp.zeros(acc_scr.shape, F32)
    nt = tk // LANES

    def chunk(j, masked):
        start = pl.multiple_of(j * tk, tk)
        k = k_ref[pl.ds(start, tk), :]
        v = v_ref[pl.ds(start, tk), :]
        alphas = []
        for c in range(2):
            rows = slice(c * tq, (c + 1) * tq)
            s = lax.dot_general(q_ref[:, c * HEAD_DIM:(c + 1) * HEAD_DIM],
                                k[:, c * HEAD_DIM:(c + 1) * HEAD_DIM],
                                _NT_DIMS, preferred_element_type=F32)
            if masked:
                row = i * tq + lax.broadcasted_iota(jnp.int32, s.shape, 0)
                col = start + lax.broadcasted_iota(jnp.int32, s.shape, 1)
                s = jnp.where(col > row, NEG_BIG, s)
            tiles = [s[:, t * LANES:(t + 1) * LANES] for t in range(nt)]
            mt = tiles[0]
            for t in range(1, nt):
                mt = jnp.maximum(mt, tiles[t])
            m_old = m_scr[rows, :]
            m_new = jnp.maximum(m_old, jnp.broadcast_to(jnp.max(mt, axis=-1, keepdims=True), m_old.shape))
            alpha = jnp.exp2(m_old - m_new)
            psum = None
            for t in range(nt):
                p = jnp.exp2(tiles[t] - m_new)
                psum = p if psum is None else psum + p
                p_scr[rows, t * LANES:(t + 1) * LANES] = p.astype(BF16)
            l_scr[rows, :] = alpha * l_scr[rows, :] + psum
            m_scr[rows, :] = m_new
            alphas.append(alpha)
        pv = jnp.dot(p_scr[...], v, preferred_element_type=F32)
        for c in range(2):
            for t in range(V_DIM // LANES):
                rows = slice(c * tq, (c + 1) * tq)
                cols = slice(t * LANES, (t + 1) * LANES)
                acc_scr[rows, cols] = alphas[c] * acc_scr[rows, cols] + pv[rows, cols]

    def body(j, carry):
        chunk(j, False)
        return carry

    per = tq // tk
    lax.fori_loop(0, i * per, body, 0)
    for jj in range(per):
        chunk(i * per + jj, True)

    lam = lam_ref[0, 0]
    inv_l = 1.0 / jnp.sum(l_scr[...], axis=-1, keepdims=True)
    inv0 = inv_l[0:tq]
    inv1 = lam * inv_l[tq:2 * tq]
    outs = []
    for t in range(V_DIM // LANES):
        cols = slice(t * LANES, (t + 1) * LANES)
        outs.append(acc_scr[0:tq, cols] * inv0 - acc_scr[tq:2 * tq, cols] * inv1)
    o = jnp.concatenate(outs, axis=1)
    ms = jnp.mean(o * o, axis=-1, keepdims=True)
    o = o * lax.rsqrt(ms + NORM_EPS) * g_ref[...] * (1.0 - LAMBDA_INIT)
    o_ref[...] = o.astype(BF16)


def _diff_attention(proj, lam, subln_g):
    s = proj.shape[0]
    tq = min(ATTN_TQ, s)
    tk = min(ATTN_TK, tq)
    kern = functools.partial(_attn_kernel, tq=tq, tk=tk)
    kblk = (2 * D_MODEL) // V_DIM // 2
    return pl.pallas_call(
        kern,
        grid=(N_HEADS, s // tq),
        in_specs=[pl.BlockSpec(memory_space=pltpu.SMEM),
                  pl.BlockSpec((tq, V_DIM), lambda h, i: (i, h)),
                  pl.BlockSpec((s, V_DIM), lambda h, i: (0, kblk + h)),
                  pl.BlockSpec((s, V_DIM), lambda h, i: (0, 2 * kblk + h)),
                  pl.BlockSpec((1, V_DIM), lambda h, i: (0, 0))],
        out_specs=pl.BlockSpec((tq, V_DIM), lambda h, i: (i, h)),
        out_shape=jax.ShapeDtypeStruct((s, N_HEADS * V_DIM), BF16),
        scratch_shapes=[pltpu.VMEM((2 * tq, LANES), F32),
                        pltpu.VMEM((2 * tq, LANES), F32),
                        pltpu.VMEM((2 * tq, V_DIM), F32),
                        pltpu.VMEM((2 * tq, tk), BF16)],
        compiler_params=_params("arbitrary", "arbitrary"),
        name="diff_attention",
    )(lam, proj, proj, proj, subln_g)


HALO = 16


def _merge_kernel(o_ref, cb_ref, cc_ref, ch_ref, hcc_ref, hch_ref, ga_ref, gc_ref, cw_ref,
                  wa_ref, wc_ref, out_ref, u_scr, pad_scr, *, tm):
    i = pl.program_id(0)
    j = pl.program_id(1)

    @pl.when(j == 0)
    def _():
        halo = hcc_ref[...].astype(F32) * hch_ref[...].astype(F32)
        pad_scr[0:HALO, :] = jnp.where(i == 0, 0.0, halo)
        up = cc_ref[...].astype(F32) * ch_ref[...].astype(F32)
        pad_scr[HALO:HALO + tm, :] = up
        cw = cw_ref[...]
        conv = (cw[0:1, :] * pad_scr[HALO - 2:HALO - 2 + tm, :]
                + cw[1:2, :] * pad_scr[HALO - 1:HALO - 1 + tm, :]
                + cw[2:3, :] * up)
        u_scr[...] = (cb_ref[...].astype(F32) * conv).astype(BF16)

    ya = jnp.dot(o_ref[...], wa_ref[...].astype(BF16), preferred_element_type=F32)
    yc = jnp.dot(u_scr[...], wc_ref[...].astype(BF16), preferred_element_type=F32)
    merged = (jax.nn.sigmoid(ga_ref[...].astype(F32)) * ya
              + jax.nn.sigmoid(gc_ref[...].astype(F32)) * yc)
    out_ref[...] = merged.astype(BF16)


def _branch_merge(o_attn, proj, conv_w, w_attn_out, w_conv_out):
    s = proj.shape[0]
    tm = min(512, s)
    tn = 512
    nb = D_MODEL // tn
    hb = tm // HALO
    kern = functools.partial(_merge_kernel, tm=tm)
    halo_idx = lambda col: (lambda i, j: (jnp.maximum(i * hb - 1, 0), col))
    return pl.pallas_call(
        kern,
        grid=(s // tm, nb),
        in_specs=[pl.BlockSpec((tm, D_MODEL), lambda i, j: (i, 0)),
                  pl.BlockSpec((tm, D_MODEL), lambda i, j: (i, 3)),
                  pl.BlockSpec((tm, D_MODEL), lambda i, j: (i, 4)),
                  pl.BlockSpec((tm, D_MODEL), lambda i, j: (i, 5)),
                  pl.BlockSpec((HALO, D_MODEL), halo_idx(4)),
                  pl.BlockSpec((HALO, D_MODEL), halo_idx(5)),
                  pl.BlockSpec((tm, tn), lambda i, j: (i, 6 * nb + j)),
                  pl.BlockSpec((tm, tn), lambda i, j: (i, 7 * nb + j)),
                  pl.BlockSpec((3, D_MODEL), lambda i, j: (0, 0)),
                  pl.BlockSpec((D_MODEL, tn), lambda i, j: (0, j)),
                  pl.BlockSpec((D_MODEL, tn), lambda i, j: (0, j))],
        out_specs=pl.BlockSpec((tm, tn), lambda i, j: (i, j)),
        out_shape=jax.ShapeDtypeStruct((s, D_MODEL), BF16),
        scratch_shapes=[pltpu.VMEM((tm, D_MODEL), BF16),
                        pltpu.VMEM((tm + HALO, D_MODEL), F32)],
        compiler_params=_params("arbitrary", "arbitrary"),
        name="branch_merge",
    )(o_attn, proj, proj, proj, proj, proj, proj, proj, conv_w, w_attn_out, w_conv_out)


def _oproj_kernel(m_ref, x_ref, g1_ref, w_ref, n2_ref, sc_ref, sh_ref, wr_ref, br_ref,
                  x1_ref, h2_ref, lg_ref, *, tn):
    j = pl.program_id(1)
    nj = pl.num_programs(1)
    col = pl.multiple_of(j * tn, tn)
    y = jnp.dot(m_ref[...], w_ref[...].astype(BF16), preferred_element_type=F32)
    x1_ref[:, pl.ds(col, tn)] = x_ref[:, pl.ds(col, tn)] + g1_ref[:, pl.ds(col, tn)] * y

    @pl.when(j == nj - 1)
    def _():
        x1 = x1_ref[...]
        ms = jnp.mean(x1 * x1, axis=-1, keepdims=True)
        h2 = x1 * lax.rsqrt(ms + NORM_EPS) * n2_ref[...] * (1.0 + sc_ref[...]) + sh_ref[...]
        h2_ref[...] = h2
        lg_ref[...] = lax.dot_general(wr_ref[...], h2, _NT_DIMS,
                                      precision=lax.Precision.HIGHEST,
                                      preferred_element_type=F32) + br_ref[...]


def _out_projection(merged, x2d, g1, w_o, n2, sc2, sh2, w_router_t, b_router_col):
    s = x2d.shape[0]
    tm = min(512, s)
    tn = 512
    kern = functools.partial(_oproj_kernel, tn=tn)
    row = lambda i, j: (0, 0)
    return pl.pallas_call(
        kern,
        grid=(s // tm, D_MODEL // tn),
        in_specs=[pl.BlockSpec((tm, D_MODEL), lambda i, j: (i, 0)),
                  pl.BlockSpec((tm, D_MODEL), lambda i, j: (i, 0)),
                  pl.BlockSpec((1, D_MODEL), row),
                  pl.BlockSpec((D_MODEL, tn), lambda i, j: (0, j)),
                  pl.BlockSpec((1, D_MODEL), row),
                  pl.BlockSpec((1, D_MODEL), row),
                  pl.BlockSpec((1, D_MODEL), row),
                  pl.BlockSpec((N_EXPERTS, D_MODEL), row),
                  pl.BlockSpec((N_EXPERTS, 1), row)],
        out_specs=[pl.BlockSpec((tm, D_MODEL), lambda i, j: (i, 0)),
                   pl.BlockSpec((tm, D_MODEL), lambda i, j: (i, 0)),
                   pl.BlockSpec((N_EXPERTS, tm), lambda i, j: (0, i))],
        out_shape=[jax.ShapeDtypeStruct((s, D_MODEL), F32),
                   jax.ShapeDtypeStruct((s, D_MODEL), F32),
                   jax.ShapeDtypeStruct((N_EXPERTS, s), F32)],
        compiler_params=_params("arbitrary", "arbitrary"),
        name="out_projection",
    )(merged, x2d, g1, w_o, n2, sc2, sh2, w_router_t, b_router_col)


ROUTE_CHUNK = 256
ROW_CHUNK = 128


def _route_kernel(lg_ref, dest_ref, wts_ref, cnt_ref, ind_scr, *, t):
    lg = lg_ref[...]
    e_iota = lax.broadcasted_iota(jnp.int32, lg.shape, 0)
    sels, vals = [], []
    for _ in range(TOP_K):
        m = jnp.max(lg, axis=0, keepdims=True)
        idx = jnp.min(jnp.where(lg == m, e_iota, N_EXPERTS), axis=0, keepdims=True)
        sel = e_iota == idx
        sels.append(sel)
        vals.append(m)
        lg = jnp.where(sel, -jnp.inf, lg)
    exps = [jnp.exp(v - vals[0]) for v in vals]
    denom = exps[0] + exps[1] + exps[2] + exps[3]
    for r in range(TOP_K):
        wts_ref[r:r + 1, :] = exps[r] / denom

    ind = jnp.zeros(lg.shape, F32)
    for sel in sels:
        ind = ind + jnp.where(sel, 1.0, 0.0)
    ind_scr[...] = ind
    counts = jnp.sum(ind, axis=1, keepdims=True)
    padded = jnp.ceil(counts / ROW_CHUNK) * ROW_CHUNK
    er = lax.broadcasted_iota(jnp.int32, (N_EXPERTS, N_EXPERTS), 0)
    ec = lax.broadcasted_iota(jnp.int32, (N_EXPERTS, N_EXPERTS), 1)
    lower = jnp.where(ec < er, 1.0, 0.0)
    pstart = jnp.dot(lower, jnp.broadcast_to(padded, (N_EXPERTS, LANES)),
                     precision=lax.Precision.HIGHEST, preferred_element_type=F32)[:, 0:1]
    cnt_ref[...] = jnp.concatenate([padded, pstart], axis=1).astype(jnp.int32)

    ur = lax.broadcasted_iota(jnp.int32, (ROUTE_CHUNK, ROUTE_CHUNK), 0)
    uc = lax.broadcasted_iota(jnp.int32, (ROUTE_CHUNK, ROUTE_CHUNK), 1)
    upper = jnp.where(ur < uc, 1.0, 0.0).astype(BF16)
    carry = pstart
    for c in range(t // ROUTE_CHUNK):
        sl = slice(c * ROUTE_CHUNK, (c + 1) * ROUTE_CHUNK)
        ind_c = ind_scr[:, sl]
        pre = jnp.dot(ind_c.astype(BF16), upper, preferred_element_type=F32) + carry
        for r in range(TOP_K):
            d = jnp.sum(jnp.where(sels[r][:, sl], pre, 0.0), axis=0, keepdims=True)
            dest_ref[r:r + 1, sl] = d.astype(jnp.int32)
        carry = carry + jnp.sum(ind_c, axis=1, keepdims=True)


def _route(logits_t):
    t = logits_t.shape[1]
    kern = functools.partial(_route_kernel, t=t)
    return pl.pallas_call(
        kern,
        out_shape=[jax.ShapeDtypeStruct((TOP_K, t), jnp.int32),
                   jax.ShapeDtypeStruct((TOP_K, t), F32),
                   jax.ShapeDtypeStruct((N_EXPERTS, 2), jnp.int32)],
        scratch_shapes=[pltpu.VMEM((N_EXPERTS, t), F32)],
        compiler_params=pltpu.CompilerParams(vmem_limit_bytes=VMEM_LIMIT_BYTES),
        name="route",
    )(logits_t)


GROUP_CAP = 2048
EXPERT_TF = 256
BIG_STEP_CHUNKS = 8


def _expert_kernel(ge_ref, gs_ref, gn_ref, ng_ref, xs_hbm, wup_ref, bup_ref, wd_ref, bd_ref, sel_ref,
                   ys_hbm, xg, acc, wub, wdb, pend, in_sem, out_sem):
    g = pl.program_id(0)
    f = pl.program_id(1)
    nf = pl.num_programs(1)
    n_chunks = gn_ref[g]
    start = pl.multiple_of(gs_ref[g], ROW_CHUNK)

    def rows_of(c):
        return pl.ds(pl.multiple_of(c * ROW_CHUNK, ROW_CHUNK), ROW_CHUNK)

    def in_copy(c):
        return pltpu.make_async_copy(
            xs_hbm.at[pl.ds(pl.multiple_of(start + c * ROW_CHUNK, ROW_CHUNK), ROW_CHUNK), :],
            xg.at[rows_of(c), :], in_sem)

    def out_copy(c, first_row):
        return pltpu.make_async_copy(
            acc.at[rows_of(c), :],
            ys_hbm.at[pl.ds(pl.multiple_of(first_row + c * ROW_CHUNK, ROW_CHUNK), ROW_CHUNK), :], out_sem)

    def drain_outputs():
        def wait_one(c, carry):
            out_copy(0, 0).wait()
            return carry
        lax.fori_loop(0, pend[0], wait_one, 0)
        pend[0] = 0

    @pl.when(jnp.logical_and(g == 0, f == 0))
    def _():
        pend[0] = 0

    @pl.when(f == 0)
    def _():
        drain_outputs()

        def start_one(c, carry):
            in_copy(c).start()
            return carry
        lax.fori_loop(0, n_chunks, start_one, 0)

        def init_one(c, carry):
            acc[rows_of(c), :] = jnp.broadcast_to(bd_ref[0], (ROW_CHUNK, D_MODEL))
            return carry
        lax.fori_loop(0, n_chunks, init_one, 0)

        def wait_one(c, carry):
            in_copy(c).wait()
            return carry
        lax.fori_loop(0, n_chunks, wait_one, 0)

    @pl.when(n_chunks > 0)
    def _():
        wub[...] = wup_ref[0].astype(BF16)
        wdb[...] = wd_ref[0].astype(BF16)
        bup = bup_ref[0]

        def rows_step(first_chunk, m):
            r = pl.ds(pl.multiple_of(first_chunk * ROW_CHUNK, ROW_CHUNK), m)
            hup = jnp.dot(xg[r, :], wub[...], preferred_element_type=F32) + bup
            parts = []
            for t in range(2 * EXPERT_TF // LANES):
                hs = hup[:, t * LANES:(t + 1) * LANES]
                lin = jnp.clip(pltpu.roll(hs, LANES - 1, 1), -SWIGLU_LIMIT, SWIGLU_LIMIT)
                glu = jnp.minimum(hs, SWIGLU_LIMIT)
                parts.append((glu * jax.nn.sigmoid(SWIGLU_ALPHA * glu) * (lin + 1.0)).astype(BF16))
            act2 = jnp.concatenate(parts, axis=1)
            act = jnp.dot(act2, sel_ref[...], preferred_element_type=F32).astype(BF16)
            acc[r, :] += jnp.dot(act, wdb[...], preferred_element_type=F32)

        n_big = n_chunks // BIG_STEP_CHUNKS

        def big_step(c, carry):
            rows_step(c * BIG_STEP_CHUNKS, BIG_STEP_CHUNKS * ROW_CHUNK)
            return carry
        lax.fori_loop(0, n_big, big_step, 0)

        size = BIG_STEP_CHUNKS // 2
        while size >= 1:
            done = (n_chunks // (2 * size)) * (2 * size)

            @pl.when((n_chunks & size) != 0)
            def _(done=done, size=size):
                rows_step(done, size * ROW_CHUNK)
            size //= 2

        @pl.when(f == nf - 1)
        def _():
            def start_one(c, carry):
                out_copy(c, start).start()
                return carry
            lax.fori_loop(0, n_chunks, start_one, 0)
            pend[0] = n_chunks

    @pl.when(jnp.logical_and(g == pl.num_programs(0) - 1, f == nf - 1))
    def _():
        drain_outputs()
        total_chunks = ys_hbm.shape[0] // ROW_CHUNK
        acc[pl.ds(0, ROW_CHUNK), :] = jnp.zeros((ROW_CHUNK, D_MODEL), F32)

        def start_one(c, carry):
            out_copy(0, c * ROW_CHUNK).start()
            return carry
        lax.fori_loop(ng_ref[1], total_chunks, start_one, 0)
        pend[0] = total_chunks - ng_ref[1]
        drain_outputs()


def _experts(xs, ge, gs, gn, ng, w_up, b_up, w_down, b_down, n_groups):
    p = xs.shape[0]
    tf = EXPERT_TF
    nf = D_EXPERT // tf
    sel = np.zeros((2 * tf, tf), np.float32)
    sel[2 * np.arange(tf), np.arange(tf)] = 1.0
    sel = jnp.asarray(sel, BF16)

    def fi(g, f, ng):
        return jnp.where(g < ng[0], f, nf - 1)

    grid_spec = pltpu.PrefetchScalarGridSpec(
        num_scalar_prefetch=4,
        grid=(n_groups, nf),
        in_specs=[pl.BlockSpec(memory_space=pl.ANY),
                  pl.BlockSpec((1, D_MODEL, 2 * tf), lambda g, f, ge, gs, gn, ng: (ge[g], 0, fi(g, f, ng))),
                  pl.BlockSpec((1, 1, 2 * tf), lambda g, f, ge, gs, gn, ng: (ge[g], 0, fi(g, f, ng))),
                  pl.BlockSpec((1, tf, D_MODEL), lambda g, f, ge, gs, gn, ng: (ge[g], fi(g, f, ng), 0)),
                  pl.BlockSpec((1, 1, D_MODEL), lambda g, f, ge, gs, gn, ng: (ge[g], 0, 0)),
                  pl.BlockSpec((2 * tf, tf), lambda g, f, ge, gs, gn, ng: (0, 0))],
        out_specs=pl.BlockSpec(memory_space=pl.ANY),
        scratch_shapes=[pltpu.VMEM((GROUP_CAP, D_MODEL), BF16),
                        pltpu.VMEM((GROUP_CAP, D_MODEL), F32),
                        pltpu.VMEM((D_MODEL, 2 * tf), BF16),
                        pltpu.VMEM((tf, D_MODEL), BF16),
                        pltpu.SMEM((1,), jnp.int32),
                        pltpu.SemaphoreType.DMA(()),
                        pltpu.SemaphoreType.DMA(())],
    )
    return pl.pallas_call(
        _expert_kernel,
        grid_spec=grid_spec,
        out_shape=jax.ShapeDtypeStruct((p, D_MODEL), F32),
        compiler_params=_params("arbitrary", "arbitrary"),
        name="experts",
    )(ge, gs, gn, ng, xs, w_up, b_up, w_down, b_down, sel)


FINAL_TM = 128


def _final_kernel(dest_ref, x1_ref, w_ref, g2_ref, fg_ref, ys_hbm, o_ref, buf, sem, *, tm):
    i = pl.program_id(0)
    n = pl.num_programs(0)

    def row_copy(row, slot, k, r):
        return pltpu.make_async_copy(ys_hbm.at[pl.ds(row, 1), :], buf.at[slot, k, pl.ds(r, 1), :], sem.at[slot])

    def issue(step, slot):
        def one(r, carry):
            for k in range(TOP_K):
                row_copy(dest_ref[(step * tm + r) * TOP_K + k], slot, k, r).start()
            return carry
        lax.fori_loop(0, tm, one, 0)

    @pl.when(i == 0)
    def _():
        issue(0, 0)

    @pl.when(i + 1 < n)
    def _():
        issue(i + 1, (i + 1) % 2)

    slot = i % 2

    def wait_one(r, carry):
        for k in range(TOP_K):
            row_copy(0, slot, k, r).wait()
        return carry
    lax.fori_loop(0, tm, wait_one, 0)

    w = w_ref[...]
    moe = w[:, 0:1] * buf[slot, 0]
    for k in range(1, TOP_K):
        moe = moe + w[:, k:k + 1] * buf[slot, k]
    x2 = x1_ref[...] + g2_ref[...] * moe
    ms = jnp.mean(x2 * x2, axis=-1, keepdims=True)
    o_ref[...] = x2 * lax.rsqrt(ms + NORM_EPS) * fg_ref[...]


def _final(dest_flat, x1, wts_t, g2, final_g, ys):
    s = x1.shape[0]
    tm = min(FINAL_TM, s)
    kern = functools.partial(_final_kernel, tm=tm)
    grid_spec = pltpu.PrefetchScalarGridSpec(
        num_scalar_prefetch=1,
        grid=(s // tm,),
        in_specs=[pl.BlockSpec((tm, D_MODEL), lambda i, d: (i, 0)),
                  pl.BlockSpec((tm, TOP_K), lambda i, d: (i, 0)),
                  pl.BlockSpec((1, D_MODEL), lambda i, d: (0, 0)),
                  pl.BlockSpec((1, D_MODEL), lambda i, d: (0, 0)),
                  pl.BlockSpec(memory_space=pl.ANY)],
        out_specs=pl.BlockSpec((tm, D_MODEL), lambda i, d: (i, 0)),
        scratch_shapes=[pltpu.VMEM((2, TOP_K, tm, D_MODEL), F32),
                        pltpu.SemaphoreType.DMA((2,))],
    )
    return pl.pallas_call(
        kern,
        grid_spec=grid_spec,
        out_shape=jax.ShapeDtypeStruct((s, D_MODEL), F32),
        compiler_params=_params("arbitrary"),
        name="final_norm",
    )(dest_flat, x1, wts_t, g2, final_g, ys)


def _lam_kernel(q1_ref, k1_ref, q2_ref, k2_ref, o_ref):
    a = jnp.sum(q1_ref[...] * k1_ref[...], axis=-1, keepdims=True)
    b = jnp.sum(q2_ref[...] * k2_ref[...], axis=-1, keepdims=True)
    o_ref[...] = jnp.exp(a) - jnp.exp(b) + LAMBDA_INIT


def _lambda(lq1, lk1, lq2, lk2):
    return pl.pallas_call(
        _lam_kernel,
        out_shape=jax.ShapeDtypeStruct((1, 1), F32),
        name="diff_lambda",
    )(lq1, lk1, lq2, lk2)


def _group_tables(padded, pstart, n_groups):
    per_e = (padded + GROUP_CAP - 1) // GROUP_CAP
    gend = jnp.cumsum(per_e)
    gidx = jnp.arange(n_groups, dtype=jnp.int32)
    e_of = jnp.minimum(jnp.sum((gend[None, :] <= gidx[:, None]).astype(jnp.int32), axis=1), N_EXPERTS - 1)
    k_of = gidx - (gend[e_of] - per_e[e_of])
    used = gidx < gend[-1]
    rows = jnp.clip(padded[e_of] - k_of * GROUP_CAP, 0, GROUP_CAP)
    last_e = jnp.max(jnp.where(per_e > 0, jnp.arange(N_EXPERTS, dtype=jnp.int32), 0))
    ge = jnp.where(used, e_of, last_e).astype(jnp.int32)
    gs = jnp.where(used, pstart[e_of] + k_of * GROUP_CAP, 0).astype(jnp.int32)
    gn = jnp.where(used, rows // ROW_CHUNK, 0).astype(jnp.int32)
    used_chunks = (pstart[-1] + padded[-1]) // ROW_CHUNK
    ng = jnp.stack([gend[-1], used_chunks]).astype(jnp.int32)
    return ge, gs, gn, ng


def kernel(x, c, positions, w_ada, b_ada, norm1_g, w_in, conv_w, lambda_q1, lambda_k1, lambda_q2, lambda_k2, subln_g, w_attn_out, w_conv_out, w_o, norm2_g, w_router, b_router, w_up, b_up, w_down, b_down, final_g):
    b, s, d = x.shape
    assert b == 1 and d == D_MODEL
    t = s
    x2d = x.reshape(t, d)

    mod = _modulation(c, w_ada[0], b_ada[0])
    sh1, sc1, g1, sh2, sc2, g2 = [mod[:, k * d:(k + 1) * d] for k in range(N_MOD)]

    inv_freq = ROPE_THETA ** (-jnp.arange(0, ROPE_DIM, 2, dtype=F32) / ROPE_DIM)
    invf = jnp.concatenate([inv_freq, inv_freq, jnp.zeros((HEAD_DIM - ROPE_DIM,), F32)]).reshape(1, HEAD_DIM)
    pos_col = positions.reshape(t, 1)

    proj = _in_projection(x2d, norm1_g, sc1, sh1, pos_col, invf, w_in[0])
    lam = _lambda(lambda_q1, lambda_k1, lambda_q2, lambda_k2)
    o_attn = _diff_attention(proj, lam, subln_g)
    merged = _branch_merge(o_attn, proj, conv_w[0], w_attn_out[0], w_conv_out[0])
    x1, h2, logits_t = _out_projection(merged, x2d, g1, w_o[0], norm2_g, sc2, sh2,
                                       w_router[0].T, b_router[0].reshape(N_EXPERTS, 1))

    dest, wts, cnt = _route(logits_t)
    a = t * TOP_K
    p = -(-(a + N_EXPERTS * (ROW_CHUNK - 1)) // ROW_CHUNK) * ROW_CHUNK
    n_groups = N_EXPERTS + -(-p // GROUP_CAP)
    ge, gs, gn, ng = _group_tables(cnt[:, 0], cnt[:, 1], n_groups)

    tok = jnp.broadcast_to(jnp.arange(t, dtype=jnp.int32)[None, :], (TOP_K, t))
    slot_tok = jnp.full((p,), t, jnp.int32).at[dest.reshape(-1)].set(tok.reshape(-1))
    h2_pad = jnp.concatenate([h2.astype(BF16), jnp.zeros((1, d), BF16)], axis=0)
    xs = h2_pad[slot_tok]

    ys = _experts(xs, ge, gs, gn, ng, w_up[0], b_up[0].reshape(N_EXPERTS, 1, 2 * D_EXPERT),
                  w_down[0], b_down[0].reshape(N_EXPERTS, 1, d), n_groups)

    out = _final(dest.T.reshape(-1), x1, wts.T, g2, final_g.reshape(1, d), ys)
    return out.reshape(b, s, d)
```

```python
import functools
import math

import numpy as np
import jax
import jax.numpy as jnp
from jax import lax
from jax.experimental import pallas as pl
from jax.experimental.pallas import tpu as pltpu

F32 = jnp.float32
BF16 = jnp.bfloat16

D_MODEL = 2048
N_HEADS = 8
HEAD_DIM = 128
V_DIM = 2 * HEAD_DIM
ROPE_DIM = 32
ROPE_THETA = 500000.0
IN_WIDTH = 8 * D_MODEL
N_EXPERTS = 32
TOP_K = 4
D_EXPERT = D_MODEL
SWIGLU_LIMIT = 7.0
SWIGLU_ALPHA = 1.702
NORM_EPS = 1e-5
N_MOD = 6
LAMBDA_INIT = 0.8 - 0.6 * math.exp(-0.3 * 0)

VMEM_LIMIT_BYTES = 56 * 1024 * 1024
NEG_BIG = -1e30
LANES = 128
Q_PRESCALE = HEAD_DIM ** -0.5 * math.log2(math.e)


def _params(*sem):
    return pltpu.CompilerParams(dimension_semantics=sem, vmem_limit_bytes=VMEM_LIMIT_BYTES)


def _mod_kernel(c_ref, w_ref, b_ref, o_ref):
    c = c_ref[...]
    c_act = c * jax.nn.sigmoid(c)
    o_ref[...] = jnp.sum(w_ref[...] * c_act, axis=0, keepdims=True) + b_ref[...]


def _modulation(c, w_ada, b_ada):
    n = w_ada.shape[1]
    tn = 1024
    return pl.pallas_call(
        _mod_kernel,
        grid=(n // tn,),
        in_specs=[pl.BlockSpec((D_MODEL, 1), lambda j: (0, 0)),
                  pl.BlockSpec((D_MODEL, tn), lambda j: (0, j)),
                  pl.BlockSpec((1, tn), lambda j: (0, j))],
        out_specs=pl.BlockSpec((1, tn), lambda j: (0, j)),
        out_shape=jax.ShapeDtypeStruct((1, n), F32),
        compiler_params=_params("arbitrary"),
        name="adaln_mod",
    )(c.reshape(D_MODEL, 1), w_ada, b_ada.reshape(1, n))


def _inproj_kernel(x_ref, g_ref, sc_ref, sh_ref, pos_ref, invf_ref, w_ref, o_ref,
                   h_scr, cos_scr, sa_scr, sb_scr, *, n_rope_tiles, tn):
    j = pl.program_id(1)

    @pl.when(j == 0)
    def _():
        x = x_ref[...]
        ms = jnp.mean(x * x, axis=-1, keepdims=True)
        y = x * lax.rsqrt(ms + NORM_EPS) * g_ref[...]
        h_scr[...] = (y * (1.0 + sc_ref[...]) + sh_ref[...]).astype(BF16)
        ang = pos_ref[...].astype(F32) * invf_ref[...]
        cs = jnp.cos(ang)
        sn = jnp.sin(ang)
        lane = lax.broadcasted_iota(jnp.int32, ang.shape, 1)
        cos_scr[...] = cs
        sa_scr[...] = jnp.where(lane >= ROPE_DIM // 2, sn, 0.0)
        sb_scr[...] = jnp.where(lane < ROPE_DIM // 2, -sn, 0.0)

    acc = jnp.dot(h_scr[...], w_ref[...].astype(BF16), preferred_element_type=F32)

    @pl.when(j < n_rope_tiles)
    def _():
        fac = jnp.where(j < n_rope_tiles // 2, Q_PRESCALE, 1.0)
        for hh in range(tn // HEAD_DIM):
            t = acc[:, hh * HEAD_DIM:(hh + 1) * HEAD_DIM]
            r = (t * cos_scr[...]
                 + pltpu.roll(t, ROPE_DIM // 2, 1) * sa_scr[...]
                 + pltpu.roll(t, HEAD_DIM - ROPE_DIM // 2, 1) * sb_scr[...])
            o_ref[:, hh * HEAD_DIM:(hh + 1) * HEAD_DIM] = (r * fac).astype(BF16)

    @pl.when(j >= n_rope_tiles)
    def _():
        o_ref[...] = acc.astype(BF16)


def _in_projection(x2d, g, sc, sh, pos_col, invf, w_in):
    s = x2d.shape[0]
    tm = min(1024, s)
    tn = 1024
    kern = functools.partial(_inproj_kernel, n_rope_tiles=(2 * D_MODEL) // tn, tn=tn)
    row = lambda i, j: (0, 0)
    return pl.pallas_call(
        kern,
        grid=(s // tm, IN_WIDTH // tn),
        in_specs=[pl.BlockSpec((tm, D_MODEL), lambda i, j: (i, 0)),
                  pl.BlockSpec((1, D_MODEL), row),
                  pl.BlockSpec((1, D_MODEL), row),
                  pl.BlockSpec((1, D_MODEL), row),
                  pl.BlockSpec((tm, 1), lambda i, j: (i, 0)),
                  pl.BlockSpec((1, HEAD_DIM), row),
                  pl.BlockSpec((D_MODEL, tn), lambda i, j: (0, j))],
        out_specs=pl.BlockSpec((tm, tn), lambda i, j: (i, j)),
        out_shape=jax.ShapeDtypeStruct((s, IN_WIDTH), BF16),
        scratch_shapes=[pltpu.VMEM((tm, D_MODEL), BF16),
                        pltpu.VMEM((tm, HEAD_DIM), F32),
                        pltpu.VMEM((tm, HEAD_DIM), F32),
                        pltpu.VMEM((tm, HEAD_DIM), F32)],
        compiler_params=_params("arbitrary", "arbitrary"),
        name="in_projection",
    )(x2d, g, sc, sh, pos_col, invf, w_in)


ATTN_TQ = 512
ATTN_TK = 512
_NT_DIMS = (((1,), (1,)), ((), ()))


def _attn_kernel(lam_ref, q_ref, k_ref, v_ref, g_ref, o_ref, m_scr, l_scr, acc_scr, p_scr, *, tq, tk):
    i = pl.program_id(1)
    m_scr[...] = jnp.full(m_scr.shape, NEG_BIG, F32)
    l_scr[...] = jnp.zeros(l_scr.shape, F32)
    acc_scr[...] = jnp.zeros(acc_scr.shape, F32)
    nt = tk // LANES

    def chunk(j, masked):
        start = pl.multiple_of(j * tk, tk)
        k = k_ref[pl.ds(start, tk), :]
        v = v_ref[pl.ds(start, tk), :]
        alphas = []
        for c in range(2):
            rows = slice(c * tq, (c + 1) * tq)
            s = lax.dot_general(q_ref[:, c * HEAD_DIM:(c + 1) * HEAD_DIM],
                                k[:, c * HEAD_DIM:(c + 1) * HEAD_DIM],
                                _NT_DIMS, preferred_element_type=F32)
            if masked:
                row = i * tq + lax.broadcasted_iota(jnp.int32, s.shape, 0)
                col = start + lax.broadcasted_iota(jnp.int32, s.shape, 1)
                s = jnp.where(col > row, NEG_BIG, s)
            tiles = [s[:, t * LANES:(t + 1) * LANES] for t in range(nt)]
            mt = tiles[0]
            for t in range(1, nt):
                mt = jnp.maximum(mt, tiles[t])
            m_old = m_scr[rows, :]
            m_new = jnp.maximum(m_old, jnp.broadcast_to(jnp.max(mt, axis=-1, keepdims=True), m_old.shape))
            alpha = jnp.exp2(m_old - m_new)
            psum = None
            for t in range(nt):
                p = jnp.exp2(tiles[t] - m_new)
                psum = p if psum is None else psum + p
                p_scr[rows, t * LANES:(t + 1) * LANES] = p.astype(BF16)
            l_scr[rows, :] = alpha * l_scr[rows, :] + psum
            m_scr[rows, :] = m_new
            alphas.append(alpha)
        pv = jnp.dot(p_scr[...], v, preferred_element_type=F32)
        for c in range(2):
            for t in range(V_DIM // LANES):
                rows = slice(c * tq, (c + 1) * tq)
                cols = slice(t * LANES, (t + 1) * LANES)
                acc_scr[rows, cols] = alphas[c] * acc_scr[rows, cols] + pv[rows, cols]

    def body(j, carry):
        chunk(j, False)
        return carry

    per = tq // tk
    lax.fori_loop(0, i * per, body, 0)
    for jj in range(per):
        chunk(i * per + jj, True)

    lam = lam_ref[0, 0]
    inv_l = 1.0 / jnp.sum(l_scr[...], axis=-1, keepdims=True)
    inv0 = inv_l[0:tq]
    inv1 = lam * inv_l[tq:2 * tq]
    outs = []
    for t in range(V_DIM // LANES):
        cols = slice(t * LANES, (t + 1) * LANES)
        outs.append(acc_scr[0:tq, cols] * inv0 - acc_scr[tq:2 * tq, cols] * inv1)
    o = jnp.concatenate(outs, axis=1)
    ms = jnp.mean(o * o, axis=-1, keepdims=True)
    o = o * lax.rsqrt(ms + NORM_EPS) * g_ref[...] * (1.0 - LAMBDA_INIT)
    o_ref[...] = o.astype(BF16)


def _diff_attention(proj, lam, subln_g):
    s = proj.shape[0]
    tq = min(ATTN_TQ, s)
    tk = min(ATTN_TK, tq)
    kern = functools.partial(_attn_kernel, tq=tq, tk=tk)
    kblk = (2 * D_MODEL) // V_DIM // 2
    return pl.pallas_call(
        kern,
        grid=(N_HEADS, s // tq),
        in_specs=[pl.BlockSpec(memory_space=pltpu.SMEM),
                  pl.BlockSpec((tq, V_DIM), lambda h, i: (i, h)),
                  pl.BlockSpec((s, V_DIM), lambda h, i: (0, kblk + h)),
                  pl.BlockSpec((s, V_DIM), lambda h, i: (0, 2 * kblk + h)),
                  pl.BlockSpec((1, V_DIM), lambda h, i: (0, 0))],
        out_specs=pl.BlockSpec((tq, V_DIM), lambda h, i: (i, h)),
        out_shape=jax.ShapeDtypeStruct((s, N_HEADS * V_DIM), BF16),
        scratch_shapes=[pltpu.VMEM((2 * tq, LANES), F32),
                        pltpu.VMEM((2 * tq, LANES), F32),
                        pltpu.VMEM((2 * tq, V_DIM), F32),
                        pltpu.VMEM((2 * tq, tk), BF16)],
        compiler_params=_params("arbitrary", "arbitrary"),
        name="diff_attention",
    )(lam, proj, proj, proj, subln_g)


HALO = 16


def _merge_kernel(o_ref, cb_ref, cc_ref, ch_ref, hcc_ref, hch_ref, ga_ref, gc_ref, cw_ref,
                  wa_ref, wc_ref, out_ref, u_scr, pad_scr, *, tm):
    i = pl.program_id(0)
    j = pl.program_id(1)

    @pl.when(j == 0)
    def _():
        halo = hcc_ref[...].astype(F32) * hch_ref[...].astype(F32)
        pad_scr[0:HALO, :] = jnp.where(i == 0, 0.0, halo)
        up = cc_ref[...].astype(F32) * ch_ref[...].astype(F32)
        pad_scr[HALO:HALO + tm, :] = up
        cw = cw_ref[...]
        conv = (cw[0:1, :] * pad_scr[HALO - 2:HALO - 2 + tm, :]
                + cw[1:2, :] * pad_scr[HALO - 1:HALO - 1 + tm, :]
                + cw[2:3, :] * up)
        u_scr[...] = (cb_ref[...].astype(F32) * conv).astype(BF16)

    ya = jnp.dot(o_ref[...], wa_ref[...], preferred_element_type=F32)
    yc = jnp.dot(u_scr[...], wc_ref[...], preferred_element_type=F32)
    merged = (jax.nn.sigmoid(ga_ref[...].astype(F32)) * ya
              + jax.nn.sigmoid(gc_ref[...].astype(F32)) * yc)
    out_ref[...] = merged.astype(BF16)


def _branch_merge(o_attn, proj, conv_w, w_attn_out, w_conv_out):
    s = proj.shape[0]
    tm = min(512, s)
    tn = 1024
    nb = D_MODEL // tn
    hb = tm // HALO
    kern = functools.partial(_merge_kernel, tm=tm)
    halo_idx = lambda col: (lambda i, j: (jnp.maximum(i * hb - 1, 0), col))
    return pl.pallas_call(
        kern,
        grid=(s // tm, nb),
        in_specs=[pl.BlockSpec((tm, D_MODEL), lambda i, j: (i, 0)),
                  pl.BlockSpec((tm, D_MODEL), lambda i, j: (i, 3)),
                  pl.BlockSpec((tm, D_MODEL), lambda i, j: (i, 4)),
                  pl.BlockSpec((tm, D_MODEL), lambda i, j: (i, 5)),
                  pl.BlockSpec((HALO, D_MODEL), halo_idx(4)),
                  pl.BlockSpec((HALO, D_MODEL), halo_idx(5)),
                  pl.BlockSpec((tm, tn), lambda i, j: (i, 6 * nb + j)),
                  pl.BlockSpec((tm, tn), lambda i, j: (i, 7 * nb + j)),
                  pl.BlockSpec((3, D_MODEL), lambda i, j: (0, 0)),
                  pl.BlockSpec((D_MODEL, tn), lambda i, j: (0, j)),
                  pl.BlockSpec((D_MODEL, tn), lambda i, j: (0, j))],
        out_specs=pl.BlockSpec((tm, tn), lambda i, j: (i, j)),
        out_shape=jax.ShapeDtypeStruct((s, D_MODEL), BF16),
        scratch_shapes=[pltpu.VMEM((tm, D_MODEL), BF16),
                        pltpu.VMEM((tm + HALO, D_MODEL), F32)],
        compiler_params=_params("arbitrary", "arbitrary"),
        name="branch_merge",
    )(o_attn, proj, proj, proj, proj, proj, proj, proj, conv_w, w_attn_out, w_conv_out)


def _oproj_kernel(m_ref, x_ref, g1_ref, w_ref, n2_ref, sc_ref, sh_ref, wr_ref, br_ref,
                  x1_ref, h2_ref, lg_ref, *, tn):
    j = pl.program_id(1)
    nj = pl.num_programs(1)
    col = pl.multiple_of(j * tn, tn)
    y = jnp.dot(m_ref[...], w_ref[...], preferred_element_type=F32)
    x1_ref[:, pl.ds(col, tn)] = x_ref[:, pl.ds(col, tn)] + g1_ref[:, pl.ds(col, tn)] * y

    @pl.when(j == nj - 1)
    def _():
        x1 = x1_ref[...]
        ms = jnp.mean(x1 * x1, axis=-1, keepdims=True)
        h2 = x1 * lax.rsqrt(ms + NORM_EPS) * n2_ref[...] * (1.0 + sc_ref[...]) + sh_ref[...]
        h2_ref[...] = h2
        lg_ref[...] = lax.dot_general(wr_ref[...], h2, _NT_DIMS,
                                      precision=lax.Precision.HIGHEST,
                                      preferred_element_type=F32) + br_ref[...]


def _out_projection(merged, x2d, g1, w_o, n2, sc2, sh2, w_router_t, b_router_col):
    s = x2d.shape[0]
    tm = min(512, s)
    tn = 1024
    kern = functools.partial(_oproj_kernel, tn=tn)
    row = lambda i, j: (0, 0)
    return pl.pallas_call(
        kern,
        grid=(s // tm, D_MODEL // tn),
        in_specs=[pl.BlockSpec((tm, D_MODEL), lambda i, j: (i, 0)),
                  pl.BlockSpec((tm, D_MODEL), lambda i, j: (i, 0)),
                  pl.BlockSpec((1, D_MODEL), row),
                  pl.BlockSpec((D_MODEL, tn), lambda i, j: (0, j)),
                  pl.BlockSpec((1, D_MODEL), row),
                  pl.BlockSpec((1, D_MODEL), row),
                  pl.BlockSpec((1, D_MODEL), row),
                  pl.BlockSpec((N_EXPERTS, D_MODEL), row),
                  pl.BlockSpec((N_EXPERTS, 1), row)],
        out_specs=[pl.BlockSpec((tm, D_MODEL), lambda i, j: (i, 0)),
                   pl.BlockSpec((tm, D_MODEL), lambda i, j: (i, 0)),
                   pl.BlockSpec((N_EXPERTS, tm), lambda i, j: (0, i))],
        out_shape=[jax.ShapeDtypeStruct((s, D_MODEL), F32),
                   jax.ShapeDtypeStruct((s, D_MODEL), F32),
                   jax.ShapeDtypeStruct((N_EXPERTS, s), F32)],
        compiler_params=_params("arbitrary", "arbitrary"),
        name="out_projection",
    )(merged, x2d, g1, w_o, n2, sc2, sh2, w_router_t, b_router_col)


ROUTE_CHUNK = 256
ROW_CHUNK = 128


def _route_kernel(lg_ref, dest_ref, wts_ref, cnt_ref, ind_scr, *, t):
    lg = lg_ref[...]
    e_iota = lax.broadcasted_iota(jnp.int32, lg.shape, 0)
    sels, vals = [], []
    for _ in range(TOP_K):
        m = jnp.max(lg, axis=0, keepdims=True)
        idx = jnp.min(jnp.where(lg == m, e_iota, N_EXPERTS), axis=0, keepdims=True)
        sel = e_iota == idx
        sels.append(sel)
        vals.append(m)
        lg = jnp.where(sel, -jnp.inf, lg)
    exps = [jnp.exp(v - vals[0]) for v in vals]
    denom = exps[0] + exps[1] + exps[2] + exps[3]
    for r in range(TOP_K):
        wts_ref[r:r + 1, :] = exps[r] / denom

    ind = jnp.zeros(lg.shape, F32)
    for sel in sels:
        ind = ind + jnp.where(sel, 1.0, 0.0)
    ind_scr[...] = ind
    counts = jnp.sum(ind, axis=1, keepdims=True)
    padded = jnp.ceil(counts / ROW_CHUNK) * ROW_CHUNK
    er = lax.broadcasted_iota(jnp.int32, (N_EXPERTS, N_EXPERTS), 0)
    ec = lax.broadcasted_iota(jnp.int32, (N_EXPERTS, N_EXPERTS), 1)
    lower = jnp.where(ec < er, 1.0, 0.0)
    pstart = jnp.dot(lower, jnp.broadcast_to(padded, (N_EXPERTS, LANES)),
                     precision=lax.Precision.HIGHEST, preferred_element_type=F32)[:, 0:1]
    cnt_ref[...] = jnp.concatenate([counts, padded, pstart], axis=1).astype(jnp.int32)

    ur = lax.broadcasted_iota(jnp.int32, (ROUTE_CHUNK, ROUTE_CHUNK), 0)
    uc = lax.broadcasted_iota(jnp.int32, (ROUTE_CHUNK, ROUTE_CHUNK), 1)
    upper = jnp.where(ur < uc, 1.0, 0.0).astype(BF16)
    carry = pstart
    for c in range(t // ROUTE_CHUNK):
        sl = slice(c * ROUTE_CHUNK, (c + 1) * ROUTE_CHUNK)
        ind_c = ind_scr[:, sl]
        pre = jnp.dot(ind_c.astype(BF16), upper, preferred_element_type=F32) + carry
        for r in range(TOP_K):
            d = jnp.sum(jnp.where(sels[r][:, sl], pre, 0.0), axis=0, keepdims=True)
            dest_ref[r:r + 1, sl] = d.astype(jnp.int32)
        carry = carry + jnp.sum(ind_c, axis=1, keepdims=True)


def _route(logits_t):
    t = logits_t.shape[1]
    kern = functools.partial(_route_kernel, t=t)
    return pl.pallas_call(
        kern,
        out_shape=[jax.ShapeDtypeStruct((TOP_K, t), jnp.int32),
                   jax.ShapeDtypeStruct((TOP_K, t), F32),
                   jax.ShapeDtypeStruct((N_EXPERTS, 3), jnp.int32)],
        scratch_shapes=[pltpu.VMEM((N_EXPERTS, t), F32)],
        compiler_params=pltpu.CompilerParams(vmem_limit_bytes=VMEM_LIMIT_BYTES),
        name="route",
    )(logits_t)


GROUP_CAP = 1536
EXPERT_TF = 256
EXPERT_TN = 512
BIG_STEP_CHUNKS = 8


def _row_steps(n_chunks, fn):
    def big_step(c, carry):
        fn(c * BIG_STEP_CHUNKS, BIG_STEP_CHUNKS * ROW_CHUNK)
        return carry
    lax.fori_loop(0, n_chunks // BIG_STEP_CHUNKS, big_step, 0)
    size = BIG_STEP_CHUNKS // 2
    while size >= 1:
        done = (n_chunks // (2 * size)) * (2 * size)

        @pl.when((n_chunks & size) != 0)
        def _(done=done, size=size):
            fn(done, size * ROW_CHUNK)
        size //= 2


def _expert_kernel(ge_ref, gs_ref, gn_ref, ng_ref, tok_ref, h_hbm, wup_ref, bup_ref, wd_ref, bd_ref, sel_ref,
                   ys_hbm, xu, xg, act, ybuf, pend, in_sem, out_sem):
    g = pl.program_id(0)
    s = pl.program_id(1)
    nfu = D_EXPERT // EXPERT_TF
    n_chunks = gn_ref[g]
    start = pl.multiple_of(gs_ref[g], ROW_CHUNK)

    def rows_of(c):
        return pl.ds(pl.multiple_of(c * ROW_CHUNK, ROW_CHUNK), ROW_CHUNK)

    def row_copy(tok, r):
        return pltpu.make_async_copy(h_hbm.at[pl.ds(tok, 1), :], xu.at[pl.ds(r, 1), :], in_sem)

    def out_copy(c, slot, first_row, col):
        return pltpu.make_async_copy(
            ybuf.at[slot, rows_of(c), :],
            ys_hbm.at[pl.ds(pl.multiple_of(first_row + c * ROW_CHUNK, ROW_CHUNK), ROW_CHUNK),
                      pl.ds(pl.multiple_of(col, EXPERT_TN), EXPERT_TN)],
            out_sem.at[slot])

    def drain(slot):
        def wait_one(c, carry):
            out_copy(0, slot, 0, 0).wait()
            return carry
        lax.fori_loop(0, pend[slot], wait_one, 0)
        pend[slot] = 0

    @pl.when(jnp.logical_and(g == 0, s == 0))
    def _():
        pend[0] = 0
        pend[1] = 0

    @pl.when(s == 0)
    def _():
        n_rows = n_chunks * ROW_CHUNK

        def start_one(r, carry):
            row_copy(tok_ref[start + r], r).start()
            return carry
        lax.fori_loop(0, n_rows, start_one, 0)

        def wait_chunk(c, carry):
            pltpu.make_async_copy(h_hbm.at[pl.ds(0, ROW_CHUNK), :], xu.at[rows_of(c), :], in_sem).wait()
            return carry
        lax.fori_loop(0, n_chunks, wait_chunk, 0)

        def cast_one(c, carry):
            xg[rows_of(c), :] = xu[rows_of(c), :].astype(BF16)
            return carry
        lax.fori_loop(0, n_chunks, cast_one, 0)

    @pl.when(jnp.logical_and(n_chunks > 0, s < nfu))
    def _():
        bup = bup_ref[0]
        col = pl.multiple_of(s * EXPERT_TF, EXPERT_TF)

        def up_step(first_chunk, m):
            r = pl.ds(pl.multiple_of(first_chunk * ROW_CHUNK, ROW_CHUNK), m)
            hup = jnp.dot(xg[r, :], wup_ref[0].astype(BF16), preferred_element_type=F32) + bup
            parts = []
            for t in range(2 * EXPERT_TF // LANES):
                hs = hup[:, t * LANES:(t + 1) * LANES]
                lin = jnp.clip(pltpu.roll(hs, LANES - 1, 1), -SWIGLU_LIMIT, SWIGLU_LIMIT)
                glu = jnp.minimum(hs, SWIGLU_LIMIT)
                parts.append((glu * jax.nn.sigmoid(SWIGLU_ALPHA * glu) * (lin + 1.0)).astype(BF16))
            act2 = jnp.concatenate(parts, axis=1)
            act[r, pl.ds(col, EXPERT_TF)] = jnp.dot(act2, sel_ref[...], preferred_element_type=F32).astype(BF16)
        _row_steps(n_chunks, up_step)

    @pl.when(jnp.logical_and(n_chunks > 0, s >= nfu))
    def _():
        bd = bd_ref[0]
        slot = (s - nfu) % 2
        drain(slot)

        def down_step(first_chunk, m):
            r = pl.ds(pl.multiple_of(first_chunk * ROW_CHUNK, ROW_CHUNK), m)
            ybuf[slot, r, :] = jnp.dot(act[r, :], wd_ref[0].astype(BF16), preferred_element_type=F32) + bd
        _row_steps(n_chunks, down_step)

        def start_one(c, carry):
            out_copy(c, slot, start, (s - nfu) * EXPERT_TN).start()
            return carry
        lax.fori_loop(0, n_chunks, start_one, 0)
        pend[slot] = n_chunks

    @pl.when(jnp.logical_and(g == pl.num_programs(0) - 1, s == pl.num_programs(1) - 1))
    def _():
        drain(0)
        drain(1)
        total_chunks = ys_hbm.shape[0] // ROW_CHUNK
        tail_chunks = total_chunks - ng_ref[1]
        ybuf[0, pl.ds(0, ROW_CHUNK), :] = jnp.zeros((ROW_CHUNK, EXPERT_TN), F32)
        for j in range(D_MODEL // EXPERT_TN):
            def start_one(c, carry, j=j):
                out_copy(0, 0, (ng_ref[1] + c) * ROW_CHUNK, j * EXPERT_TN).start()
                return carry
            lax.fori_loop(0, tail_chunks, start_one, 0)
        pend[0] = tail_chunks * (D_MODEL // EXPERT_TN)
        drain(0)


def _experts(h2, slot_tok, ge, gs, gn, ng, w_up, b_up, w_down, b_down, n_groups):
    p = slot_tok.shape[0]
    tf, tn = EXPERT_TF, EXPERT_TN
    nfu, nfd = D_EXPERT // tf, D_MODEL // tn
    sel = np.zeros((2 * tf, tf), np.float32)
    sel[2 * np.arange(tf), np.arange(tf)] = 1.0
    sel = jnp.asarray(sel, BF16)

    def up_i(g, s, ng):
        return jnp.where(g < ng[0], jnp.minimum(s, nfu - 1), nfu - 1)

    def dn_i(g, s, ng):
        return jnp.where(g < ng[0], jnp.clip(s - nfu, 0, nfd - 1), nfd - 1)

    grid_spec = pltpu.PrefetchScalarGridSpec(
        num_scalar_prefetch=5,
        grid=(n_groups, nfu + nfd),
        in_specs=[pl.BlockSpec(memory_space=pl.ANY),
                  pl.BlockSpec((1, D_MODEL, 2 * tf), lambda g, s, ge, gs, gn, ng, tk: (ge[g], 0, up_i(g, s, ng))),
                  pl.BlockSpec((1, 1, 2 * tf), lambda g, s, ge, gs, gn, ng, tk: (ge[g], 0, up_i(g, s, ng))),
                  pl.BlockSpec((1, D_EXPERT, tn), lambda g, s, ge, gs, gn, ng, tk: (ge[g], 0, dn_i(g, s, ng))),
                  pl.BlockSpec((1, 1, tn), lambda g, s, ge, gs, gn, ng, tk: (ge[g], 0, dn_i(g, s, ng))),
                  pl.BlockSpec((2 * tf, tf), lambda g, s, ge, gs, gn, ng, tk: (0, 0))],
        out_specs=pl.BlockSpec(memory_space=pl.ANY),
        scratch_shapes=[pltpu.VMEM((GROUP_CAP, D_MODEL), F32),
                        pltpu.VMEM((GROUP_CAP, D_MODEL), BF16),
                        pltpu.VMEM((GROUP_CAP, D_EXPERT), BF16),
                        pltpu.VMEM((2, GROUP_CAP, tn), F32),
                        pltpu.SMEM((2,), jnp.int32),
                        pltpu.SemaphoreType.DMA(()),
                        pltpu.SemaphoreType.DMA((2,))],
    )
    return pl.pallas_call(
        _expert_kernel,
        grid_spec=grid_spec,
        out_shape=jax.ShapeDtypeStruct((p, D_MODEL), F32),
        compiler_params=_params("arbitrary", "arbitrary"),
        name="experts",
    )(ge, gs, gn, ng, slot_tok, h2, w_up, b_up, w_down, b_down, sel)


FINAL_TM = 128


def _final_kernel(dest_ref, x1_ref, w_ref, g2_ref, fg_ref, ys_hbm, o_ref, buf, sem, *, tm):
    i = pl.program_id(0)
    n = pl.num_programs(0)

    def row_copy(row, slot, k, r):
        return pltpu.make_async_copy(ys_hbm.at[pl.ds(row, 1), :], buf.at[slot, k, pl.ds(r, 1), :], sem.at[slot])

    def issue(step, slot):
        def one(r, carry):
            for k in range(TOP_K):
                row_copy(dest_ref[(step * tm + r) * TOP_K + k], slot, k, r).start()
            return carry
        lax.fori_loop(0, tm, one, 0)

    @pl.when(i == 0)
    def _():
        issue(0, 0)

    @pl.when(i + 1 < n)
    def _():
        issue(i + 1, (i + 1) % 2)

    slot = i % 2

    for k in range(TOP_K):
        pltpu.make_async_copy(ys_hbm.at[pl.ds(0, tm), :], buf.at[slot, k], sem.at[slot]).wait()

    w = w_ref[...]
    moe = w[:, 0:1] * buf[slot, 0]
    for k in range(1, TOP_K):
        moe = moe + w[:, k:k + 1] * buf[slot, k]
    x2 = x1_ref[...] + g2_ref[...] * moe
    ms = jnp.mean(x2 * x2, axis=-1, keepdims=True)
    o_ref[...] = x2 * lax.rsqrt(ms + NORM_EPS) * fg_ref[...]


def _final(dest_flat, x1, wts_t, g2, final_g, ys):
    s = x1.shape[0]
    tm = min(FINAL_TM, s)
    kern = functools.partial(_final_kernel, tm=tm)
    grid_spec = pltpu.PrefetchScalarGridSpec(
        num_scalar_prefetch=1,
        grid=(s // tm,),
        in_specs=[pl.BlockSpec((tm, D_MODEL), lambda i, d: (i, 0)),
                  pl.BlockSpec((tm, TOP_K), lambda i, d: (i, 0)),
                  pl.BlockSpec((1, D_MODEL), lambda i, d: (0, 0)),
                  pl.BlockSpec((1, D_MODEL), lambda i, d: (0, 0)),
                  pl.BlockSpec(memory_space=pl.ANY)],
        out_specs=pl.BlockSpec((tm, D_MODEL), lambda i, d: (i, 0)),
        scratch_shapes=[pltpu.VMEM((2, TOP_K, tm, D_MODEL), F32),
                        pltpu.SemaphoreType.DMA((2,))],
    )
    return pl.pallas_call(
        kern,
        grid_spec=grid_spec,
        out_shape=jax.ShapeDtypeStruct((s, D_MODEL), F32),
        compiler_params=_params("arbitrary"),
        name="final_norm",
    )(dest_flat, x1, wts_t, g2, final_g, ys)


def _lam_kernel(q1_ref, k1_ref, q2_ref, k2_ref, o_ref):
    a = jnp.sum(q1_ref[...] * k1_ref[...], axis=-1, keepdims=True)
    b = jnp.sum(q2_ref[...] * k2_ref[...], axis=-1, keepdims=True)
    o_ref[...] = jnp.exp(a) - jnp.exp(b) + LAMBDA_INIT


def _lambda(lq1, lk1, lq2, lk2):
    return pl.pallas_call(
        _lam_kernel,
        out_shape=jax.ShapeDtypeStruct((1, 1), F32),
        name="diff_lambda",
    )(lq1, lk1, lq2, lk2)


def _group_tables(padded, pstart, n_groups):
    per_e = (padded + GROUP_CAP - 1) // GROUP_CAP
    gend = jnp.cumsum(per_e)
    gidx = jnp.arange(n_groups, dtype=jnp.int32)
    e_of = jnp.minimum(jnp.sum((gend[None, :] <= gidx[:, None]).astype(jnp.int32), axis=1), N_EXPERTS - 1)
    k_of = gidx - (gend[e_of] - per_e[e_of])
    used = gidx < gend[-1]
    rows = jnp.clip(padded[e_of] - k_of * GROUP_CAP, 0, GROUP_CAP)
    last_e = jnp.max(jnp.where(per_e > 0, jnp.arange(N_EXPERTS, dtype=jnp.int32), 0))
    ge = jnp.where(used, e_of, last_e).astype(jnp.int32)
    gs = jnp.where(used, pstart[e_of] + k_of * GROUP_CAP, 0).astype(jnp.int32)
    gn = jnp.where(used, rows // ROW_CHUNK, 0).astype(jnp.int32)
    used_chunks = (pstart[-1] + padded[-1]) // ROW_CHUNK
    ng = jnp.stack([gend[-1], used_chunks]).astype(jnp.int32)
    return ge, gs, gn, ng


def kernel(x, c, positions, w_ada, b_ada, norm1_g, w_in, conv_w, lambda_q1, lambda_k1, lambda_q2, lambda_k2, subln_g, w_attn_out, w_conv_out, w_o, norm2_g, w_router, b_router, w_up, b_up, w_down, b_down, final_g):
    b, s, d = x.shape
    assert b == 1 and d == D_MODEL
    t = s
    x2d = x.reshape(t, d)

    mod = _modulation(c, w_ada[0], b_ada[0])
    sh1, sc1, g1, sh2, sc2, g2 = [mod[:, k * d:(k + 1) * d] for k in range(N_MOD)]

    inv_freq = ROPE_THETA ** (-jnp.arange(0, ROPE_DIM, 2, dtype=F32) / ROPE_DIM)
    invf = jnp.concatenate([inv_freq, inv_freq, jnp.zeros((HEAD_DIM - ROPE_DIM,), F32)]).reshape(1, HEAD_DIM)
    pos_col = positions.reshape(t, 1)

    proj = _in_projection(x2d, norm1_g, sc1, sh1, pos_col, invf, w_in[0])
    lam = _lambda(lambda_q1, lambda_k1, lambda_q2, lambda_k2)
    o_attn = _diff_attention(proj, lam, subln_g)
    merged = _branch_merge(o_attn, proj, conv_w[0], w_attn_out[0].astype(BF16), w_conv_out[0].astype(BF16))
    x1, h2, logits_t = _out_projection(merged, x2d, g1, w_o[0].astype(BF16), norm2_g, sc2, sh2,
                                        w_router[0].T, b_router[0].reshape(N_EXPERTS, 1))

    dest, wts, cnt = _route(logits_t)
    a = t * TOP_K
    p = -(-(a + N_EXPERTS * (ROW_CHUNK - 1)) // ROW_CHUNK) * ROW_CHUNK
    n_groups = N_EXPERTS + -(-p // GROUP_CAP)
    ge, gs, gn, ng = _group_tables(cnt[:, 1], cnt[:, 2], n_groups)
    dest_flat = dest.T.reshape(-1)

    slot_tok = jnp.zeros((p,), jnp.int32).at[dest_flat].set(jnp.arange(a, dtype=jnp.int32) // TOP_K)
    ys = _experts(h2, slot_tok, ge, gs, gn, ng, w_up[0], b_up[0].reshape(N_EXPERTS, 1, 2 * D_EXPERT),
                  w_down[0], b_down[0].reshape(N_EXPERTS, 1, d), n_groups)

    out = _final(dest_flat, x1, wts.T, g2, final_g.reshape(1, d), ys)
    return out.reshape(b, s, d)
```

```python
import functools
import math

import numpy as np
import jax
import jax.numpy as jnp
from jax import lax
from jax.experimental import pallas as pl
from jax.experimental.pallas import tpu as pltpu

F32 = jnp.float32
BF16 = jnp.bfloat16

D_MODEL = 2048
N_HEADS = 8
HEAD_DIM = 128
V_DIM = 2 * HEAD_DIM
ROPE_DIM = 32
ROPE_THETA = 500000.0
IN_WIDTH = 8 * D_MODEL
N_EXPERTS = 32
TOP_K = 4
D_EXPERT = D_MODEL
SWIGLU_LIMIT = 7.0
SWIGLU_ALPHA = 1.702
NORM_EPS = 1e-5
N_MOD = 6
LAMBDA_INIT = 0.8 - 0.6 * math.exp(-0.3 * 0)

VMEM_LIMIT_BYTES = 56 * 1024 * 1024
NEG_BIG = -1e30
LANES = 128
Q_PRESCALE = HEAD_DIM ** -0.5 * math.log2(math.e)


def _params(*sem):
    return pltpu.CompilerParams(dimension_semantics=sem, vmem_limit_bytes=VMEM_LIMIT_BYTES)


def _mod_kernel(c_ref, w_ref, b_ref, o_ref):
    c = c_ref[...]
    c_act = c * jax.nn.sigmoid(c)
    o_ref[...] = jnp.sum(w_ref[...] * c_act, axis=0, keepdims=True) + b_ref[...]


def _modulation(c, w_ada, b_ada):
    n = w_ada.shape[1]
    tn = 1024
    return pl.pallas_call(
        _mod_kernel,
        grid=(n // tn,),
        in_specs=[pl.BlockSpec((D_MODEL, 1), lambda j: (0, 0)),
                  pl.BlockSpec((D_MODEL, tn), lambda j: (0, j)),
                  pl.BlockSpec((1, tn), lambda j: (0, j))],
        out_specs=pl.BlockSpec((1, tn), lambda j: (0, j)),
        out_shape=jax.ShapeDtypeStruct((1, n), F32),
        compiler_params=_params("arbitrary"),
        name="adaln_mod",
    )(c.reshape(D_MODEL, 1), w_ada, b_ada.reshape(1, n))


def _inproj_kernel(x_ref, g_ref, sc_ref, sh_ref, pos_ref, invf_ref, w_ref, o_ref,
                   h_scr, cos_scr, sa_scr, sb_scr, *, n_rope_tiles, tn):
    j = pl.program_id(1)

    @pl.when(j == 0)
    def _():
        x = x_ref[...]
        ms = jnp.mean(x * x, axis=-1, keepdims=True)
        y = x * lax.rsqrt(ms + NORM_EPS) * g_ref[...]
        h_scr[...] = (y * (1.0 + sc_ref[...]) + sh_ref[...]).astype(BF16)
        ang = pos_ref[...].astype(F32) * invf_ref[...]
        cs = jnp.cos(ang)
        sn = jnp.sin(ang)
        lane = lax.broadcasted_iota(jnp.int32, ang.shape, 1)
        cos_scr[...] = cs
        sa_scr[...] = jnp.where(lane >= ROPE_DIM // 2, sn, 0.0)
        sb_scr[...] = jnp.where(lane < ROPE_DIM // 2, -sn, 0.0)

    acc = jnp.dot(h_scr[...], w_ref[...].astype(BF16), preferred_element_type=F32)

    @pl.when(j < n_rope_tiles)
    def _():
        fac = jnp.where(j < n_rope_tiles // 2, Q_PRESCALE, 1.0)
        for hh in range(tn // HEAD_DIM):
            t = acc[:, hh * HEAD_DIM:(hh + 1) * HEAD_DIM]
            r = (t * cos_scr[...]
                 + pltpu.roll(t, ROPE_DIM // 2, 1) * sa_scr[...]
                 + pltpu.roll(t, HEAD_DIM - ROPE_DIM // 2, 1) * sb_scr[...])
            o_ref[:, hh * HEAD_DIM:(hh + 1) * HEAD_DIM] = (r * fac).astype(BF16)

    @pl.when(j >= n_rope_tiles)
    def _():
        o_ref[...] = acc.astype(BF16)


def _in_projection(x2d, g, sc, sh, pos_col, invf, w_in):
    s = x2d.shape[0]
    tm = min(1024, s)
    tn = 1024
    kern = functools.partial(_inproj_kernel, n_rope_tiles=(2 * D_MODEL) // tn, tn=tn)
    row = lambda i, j: (0, 0)
    return pl.pallas_call(
        kern,
        grid=(s // tm, IN_WIDTH // tn),
        in_specs=[pl.BlockSpec((tm, D_MODEL), lambda i, j: (i, 0)),
                  pl.BlockSpec((1, D_MODEL), row),
                  pl.BlockSpec((1, D_MODEL), row),
                  pl.BlockSpec((1, D_MODEL), row),
                  pl.BlockSpec((tm, 1), lambda i, j: (i, 0)),
                  pl.BlockSpec((1, HEAD_DIM), row),
                  pl.BlockSpec((D_MODEL, tn), lambda i, j: (0, j))],
        out_specs=pl.BlockSpec((tm, tn), lambda i, j: (i, j)),
        out_shape=jax.ShapeDtypeStruct((s, IN_WIDTH), BF16),
        scratch_shapes=[pltpu.VMEM((tm, D_MODEL), BF16),
                        pltpu.VMEM((tm, HEAD_DIM), F32),
                        pltpu.VMEM((tm, HEAD_DIM), F32),
                        pltpu.VMEM((tm, HEAD_DIM), F32)],
        compiler_params=_params("arbitrary", "arbitrary"),
        name="in_projection",
    )(x2d, g, sc, sh, pos_col, invf, w_in)


ATTN_TQ = 512
ATTN_TK = 512
_NT_DIMS = (((1,), (1,)), ((), ()))


def _attn_kernel(lam_ref, q_ref, k_ref, v_ref, g_ref, o_ref, m_scr, l_scr, acc_scr, p_scr, *, tq, tk):
    i = pl.program_id(1)
    m_scr[...] = jnp.full(m_scr.shape, NEG_BIG, F32)
    l_scr[...] = jnp.zeros(l_scr.shape, F32)
    acc_scr[...] = jnp.zeros(acc_scr.shape, F32)
    nt = tk // LANES

    def chunk(j, masked):
        start = pl.multiple_of(j * tk, tk)
        k = k_ref[pl.ds(start, tk), :]
        v = v_ref[pl.ds(start, tk), :]
        alphas = []
        for c in range(2):
            rows = slice(c * tq, (c + 1) * tq)
            s = lax.dot_general(q_ref[:, c * HEAD_DIM:(c + 1) * HEAD_DIM],
                                k[:, c * HEAD_DIM:(c + 1) * HEAD_DIM],
                                _NT_DIMS, preferred_element_type=F32)
            if masked:
                row = i * tq + lax.broadcasted_iota(jnp.int32, s.shape, 0)
                col = start + lax.broadcasted_iota(jnp.int32, s.shape, 1)
                s = jnp.where(col > row, NEG_BIG, s)
            tiles = [s[:, t * LANES:(t + 1) * LANES] for t in range(nt)]
            mt = tiles[0]
            for t in range(1, nt):
                mt = jnp.maximum(mt, tiles[t])
            m_old = m_scr[rows, :]
            m_new = jnp.maximum(m_old, jnp.broadcast_to(jnp.max(mt, axis=-1, keepdims=True), m_old.shape))
            alpha = jnp.exp2(m_old - m_new)
            psum = None
            for t in range(nt):
                p = jnp.exp2(tiles[t] - m_new)
                psum = p if psum is None else psum + p
                p_scr[rows, t * LANES:(t + 1) * LANES] = p.astype(BF16)
            l_scr[rows, :] = alpha * l_scr[rows, :] + psum
            m_scr[rows, :] = m_new
            alphas.append(alpha)
        pv = jnp.dot(p_scr[...], v, preferred_element_type=F32)
        for c in range(2):
            for t in range(V_DIM // LANES):
                rows = slice(c * tq, (c + 1) * tq)
                cols = slice(t * LANES, (t + 1) * LANES)
                acc_scr[rows, cols] = alphas[c] * acc_scr[rows, cols] + pv[rows, cols]

    def body(j, carry):
        chunk(j, False)
        return carry

    per = tq // tk
    lax.fori_loop(0, i * per, body, 0)
    for jj in range(per):
        chunk(i * per + jj, True)

    lam = lam_ref[0, 0]
    inv_l = 1.0 / jnp.sum(l_scr[...], axis=-1, keepdims=True)
    inv0 = inv_l[0:tq]
    inv1 = lam * inv_l[tq:2 * tq]
    outs = []
    for t in range(V_DIM // LANES):
        cols = slice(t * LANES, (t + 1) * LANES)
        outs.append(acc_scr[0:tq, cols] * inv0 - acc_scr[tq:2 * tq, cols] * inv1)
    o = jnp.concatenate(outs, axis=1)
    ms = jnp.mean(o * o, axis=-1, keepdims=True)
    o = o * lax.rsqrt(ms + NORM_EPS) * g_ref[...] * (1.0 - LAMBDA_INIT)
    o_ref[...] = o.astype(BF16)


def _diff_attention(proj, lam, subln_g):
    s = proj.shape[0]
    tq = min(ATTN_TQ, s)
    tk = min(ATTN_TK, tq)
    kern = functools.partial(_attn_kernel, tq=tq, tk=tk)
    kblk = (2 * D_MODEL) // V_DIM // 2
    return pl.pallas_call(
        kern,
        grid=(N_HEADS, s // tq),
        in_specs=[pl.BlockSpec(memory_space=pltpu.SMEM),
                  pl.BlockSpec((tq, V_DIM), lambda h, i: (i, h)),
                  pl.BlockSpec((s, V_DIM), lambda h, i: (0, kblk + h)),
                  pl.BlockSpec((s, V_DIM), lambda h, i: (0, 2 * kblk + h)),
                  pl.BlockSpec((1, V_DIM), lambda h, i: (0, 0))],
        out_specs=pl.BlockSpec((tq, V_DIM), lambda h, i: (i, h)),
        out_shape=jax.ShapeDtypeStruct((s, N_HEADS * V_DIM), BF16),
        scratch_shapes=[pltpu.VMEM((2 * tq, LANES), F32),
                        pltpu.VMEM((2 * tq, LANES), F32),
                        pltpu.VMEM((2 * tq, V_DIM), F32),
                        pltpu.VMEM((2 * tq, tk), BF16)],
        compiler_params=_params("arbitrary", "arbitrary"),
        name="diff_attention",
    )(lam, proj, proj, proj, subln_g)


HALO = 16


def _merge_kernel(o_ref, cb_ref, cc_ref, ch_ref, hcc_ref, hch_ref, ga_ref, gc_ref, cw_ref,
                  wa_ref, wc_ref, out_ref, u_scr, pad_scr, *, tm):
    i = pl.program_id(0)
    j = pl.program_id(1)

    @pl.when(j == 0)
    def _():
        halo = hcc_ref[...].astype(F32) * hch_ref[...].astype(F32)
        pad_scr[0:HALO, :] = jnp.where(i == 0, 0.0, halo)
        up = cc_ref[...].astype(F32) * ch_ref[...].astype(F32)
        pad_scr[HALO:HALO + tm, :] = up
        cw = cw_ref[...]
        conv = (cw[0:1, :] * pad_scr[HALO - 2:HALO - 2 + tm, :]
                + cw[1:2, :] * pad_scr[HALO - 1:HALO - 1 + tm, :]
                + cw[2:3, :] * up)
        u_scr[...] = (cb_ref[...].astype(F32) * conv).astype(BF16)

    ya = jnp.dot(o_ref[...], wa_ref[...], preferred_element_type=F32)
    yc = jnp.dot(u_scr[...], wc_ref[...], preferred_element_type=F32)
    merged = (jax.nn.sigmoid(ga_ref[...].astype(F32)) * ya
              + jax.nn.sigmoid(gc_ref[...].astype(F32)) * yc)
    out_ref[...] = merged.astype(BF16)


def _branch_merge(o_attn, proj, conv_w, w_attn_out, w_conv_out):
    s = proj.shape[0]
    tm = min(512, s)
    tn = 1024
    nb = D_MODEL // tn
    hb = tm // HALO
    kern = functools.partial(_merge_kernel, tm=tm)
    halo_idx = lambda col: (lambda i, j: (jnp.maximum(i * hb - 1, 0), col))
    return pl.pallas_call(
        kern,
        grid=(s // tm, nb),
        in_specs=[pl.BlockSpec((tm, D_MODEL), lambda i, j: (i, 0)),
                  pl.BlockSpec((tm, D_MODEL), lambda i, j: (i, 3)),
                  pl.BlockSpec((tm, D_MODEL), lambda i, j: (i, 4)),
                  pl.BlockSpec((tm, D_MODEL), lambda i, j: (i, 5)),
                  pl.BlockSpec((HALO, D_MODEL), halo_idx(4)),
                  pl.BlockSpec((HALO, D_MODEL), halo_idx(5)),
                  pl.BlockSpec((tm, tn), lambda i, j: (i, 6 * nb + j)),
                  pl.BlockSpec((tm, tn), lambda i, j: (i, 7 * nb + j)),
                  pl.BlockSpec((3, D_MODEL), lambda i, j: (0, 0)),
                  pl.BlockSpec((D_MODEL, tn), lambda i, j: (0, j)),
                  pl.BlockSpec((D_MODEL, tn), lambda i, j: (0, j))],
        out_specs=pl.BlockSpec((tm, tn), lambda i, j: (i, j)),
        out_shape=jax.ShapeDtypeStruct((s, D_MODEL), BF16),
        scratch_shapes=[pltpu.VMEM((tm, D_MODEL), BF16),
                        pltpu.VMEM((tm + HALO, D_MODEL), F32)],
        compiler_params=_params("arbitrary", "arbitrary"),
        name="branch_merge",
    )(o_attn, proj, proj, proj, proj, proj, proj, proj, conv_w, w_attn_out, w_conv_out)


def _oproj_kernel(m_ref, x_ref, g1_ref, w_ref, n2_ref, sc_ref, sh_ref, wr_ref, br_ref,
                  x1_ref, h2_ref, lg_ref, *, tn):
    j = pl.program_id(1)
    nj = pl.num_programs(1)
    col = pl.multiple_of(j * tn, tn)
    y = jnp.dot(m_ref[...], w_ref[...], preferred_element_type=F32)
    x1_ref[:, pl.ds(col, tn)] = x_ref[:, pl.ds(col, tn)] + g1_ref[:, pl.ds(col, tn)] * y

    @pl.when(j == nj - 1)
    def _():
        x1 = x1_ref[...]
        ms = jnp.mean(x1 * x1, axis=-1, keepdims=True)
        h2 = x1 * lax.rsqrt(ms + NORM_EPS) * n2_ref[...] * (1.0 + sc_ref[...]) + sh_ref[...]
        h2_ref[...] = h2
        lg_ref[...] = lax.dot_general(wr_ref[...], h2, _NT_DIMS,
                                      precision=lax.Precision.HIGHEST,
                                      preferred_element_type=F32) + br_ref[...]


def _out_projection(merged, x2d, g1, w_o, n2, sc2, sh2, w_router_t, b_router_col):
    s = x2d.shape[0]
    tm = min(512, s)
    tn = 1024
    kern = functools.partial(_oproj_kernel, tn=tn)
    row = lambda i, j: (0, 0)
    return pl.pallas_call(
        kern,
        grid=(s // tm, D_MODEL // tn),
        in_specs=[pl.BlockSpec((tm, D_MODEL), lambda i, j: (i, 0)),
                  pl.BlockSpec((tm, D_MODEL), lambda i, j: (i, 0)),
                  pl.BlockSpec((1, D_MODEL), row),
                  pl.BlockSpec((D_MODEL, tn), lambda i, j: (0, j)),
                  pl.BlockSpec((1, D_MODEL), row),
                  pl.BlockSpec((1, D_MODEL), row),
                  pl.BlockSpec((1, D_MODEL), row),
                  pl.BlockSpec((N_EXPERTS, D_MODEL), row),
                  pl.BlockSpec((N_EXPERTS, 1), row)],
        out_specs=[pl.BlockSpec((tm, D_MODEL), lambda i, j: (i, 0)),
                   pl.BlockSpec((tm, D_MODEL), lambda i, j: (i, 0)),
                   pl.BlockSpec((N_EXPERTS, tm), lambda i, j: (0, i))],
        out_shape=[jax.ShapeDtypeStruct((s, D_MODEL), F32),
                   jax.ShapeDtypeStruct((s, D_MODEL), F32),
                   jax.ShapeDtypeStruct((N_EXPERTS, s), F32)],
        compiler_params=_params("arbitrary", "arbitrary"),
        name="out_projection",
    )(merged, x2d, g1, w_o, n2, sc2, sh2, w_router_t, b_router_col)


ROUTE_CHUNK = 256
ROW_CHUNK = 128


def _route_kernel(lg_ref, dest_ref, wts_ref, cnt_ref, ind_scr, *, t):
    lg = lg_ref[...]
    e_iota = lax.broadcasted_iota(jnp.int32, lg.shape, 0)
    sels, vals = [], []
    for _ in range(TOP_K):
        m = jnp.max(lg, axis=0, keepdims=True)
        idx = jnp.min(jnp.where(lg == m, e_iota, N_EXPERTS), axis=0, keepdims=True)
        sel = e_iota == idx
        sels.append(sel)
        vals.append(m)
        lg = jnp.where(sel, -jnp.inf, lg)
    exps = [jnp.exp(v - vals[0]) for v in vals]
    denom = exps[0] + exps[1] + exps[2] + exps[3]
    for r in range(TOP_K):
        wts_ref[r:r + 1, :] = exps[r] / denom

    ind = jnp.zeros(lg.shape, F32)
    for sel in sels:
        ind = ind + jnp.where(sel, 1.0, 0.0)
    ind_scr[...] = ind
    counts = jnp.sum(ind, axis=1, keepdims=True)
    padded = jnp.ceil(counts / ROW_CHUNK) * ROW_CHUNK
    er = lax.broadcasted_iota(jnp.int32, (N_EXPERTS, N_EXPERTS), 0)
    ec = lax.broadcasted_iota(jnp.int32, (N_EXPERTS, N_EXPERTS), 1)
    lower = jnp.where(ec < er, 1.0, 0.0)
    pstart = jnp.dot(lower, jnp.broadcast_to(padded, (N_EXPERTS, LANES)),
                     precision=lax.Precision.HIGHEST, preferred_element_type=F32)[:, 0:1]
    cnt_ref[...] = jnp.concatenate([counts, padded, pstart], axis=1).astype(jnp.int32)

    ur = lax.broadcasted_iota(jnp.int32, (ROUTE_CHUNK, ROUTE_CHUNK), 0)
    uc = lax.broadcasted_iota(jnp.int32, (ROUTE_CHUNK, ROUTE_CHUNK), 1)
    upper = jnp.where(ur < uc, 1.0, 0.0).astype(BF16)
    carry = pstart
    for c in range(t // ROUTE_CHUNK):
        sl = slice(c * ROUTE_CHUNK, (c + 1) * ROUTE_CHUNK)
        ind_c = ind_scr[:, sl]
        pre = jnp.dot(ind_c.astype(BF16), upper, preferred_element_type=F32) + carry
        for r in range(TOP_K):
            d = jnp.sum(jnp.where(sels[r][:, sl], pre, 0.0), axis=0, keepdims=True)
            dest_ref[r:r + 1, sl] = d.astype(jnp.int32)
        carry = carry + jnp.sum(ind_c, axis=1, keepdims=True)


def _route(logits_t):
    t = logits_t.shape[1]
    kern = functools.partial(_route_kernel, t=t)
    return pl.pallas_call(
        kern,
        out_shape=[jax.ShapeDtypeStruct((TOP_K, t), jnp.int32),
                   jax.ShapeDtypeStruct((TOP_K, t), F32),
                   jax.ShapeDtypeStruct((N_EXPERTS, 3), jnp.int32)],
        scratch_shapes=[pltpu.VMEM((N_EXPERTS, t), F32)],
        compiler_params=pltpu.CompilerParams(vmem_limit_bytes=VMEM_LIMIT_BYTES),
        name="route",
    )(logits_t)


GROUP_CAP = 1536
EXPERT_TF = 256
EXPERT_TN = 512
BIG_STEP_CHUNKS = 8
ISSUE_UNROLL = 8


def _row_steps(n_chunks, fn):
    def big_step(c, carry):
        fn(c * BIG_STEP_CHUNKS, BIG_STEP_CHUNKS * ROW_CHUNK)
        return carry
    lax.fori_loop(0, n_chunks // BIG_STEP_CHUNKS, big_step, 0)
    size = BIG_STEP_CHUNKS // 2
    while size >= 1:
        done = (n_chunks // (2 * size)) * (2 * size)

        @pl.when((n_chunks & size) != 0)
        def _(done=done, size=size):
            fn(done, size * ROW_CHUNK)
        size //= 2


def _expert_kernel(ge_ref, gs_ref, gn_ref, ng_ref, dest_ref, h_hbm, wup_ref, bup_ref, wd_ref, bd_ref, sel_ref,
                   ys_hbm, xu, xg, act, ybuf, tok_ref, pend, in_sem, out_sem):
    g = pl.program_id(0)
    s = pl.program_id(1)
    nfu = D_EXPERT // EXPERT_TF
    n_chunks = gn_ref[g]
    start = pl.multiple_of(gs_ref[g], ROW_CHUNK)

    def rows_of(c):
        return pl.ds(pl.multiple_of(c * ROW_CHUNK, ROW_CHUNK), ROW_CHUNK)

    def row_copy(tok, r):
        return pltpu.make_async_copy(h_hbm.at[pl.ds(tok, 1), :], xu.at[pl.ds(r, 1), :], in_sem)

    def out_copy(c, slot, first_row, col):
        return pltpu.make_async_copy(
            ybuf.at[slot, rows_of(c), :],
            ys_hbm.at[pl.ds(pl.multiple_of(first_row + c * ROW_CHUNK, ROW_CHUNK), ROW_CHUNK),
                      pl.ds(pl.multiple_of(col, EXPERT_TN), EXPERT_TN)],
            out_sem.at[slot])

    def drain(slot):
        def wait_one(c, carry):
            out_copy(0, slot, 0, 0).wait()
            return carry
        lax.fori_loop(0, pend[slot], wait_one, 0)
        pend[slot] = 0

    @pl.when(jnp.logical_and(g == 0, s == 0))
    def _():
        pend[0] = 0
        pend[1] = 0

        def clear_some(b, carry):
            for u in range(ISSUE_UNROLL):
                tok_ref[b * ISSUE_UNROLL + u] = 0
            return carry
        lax.fori_loop(0, tok_ref.shape[0] // ISSUE_UNROLL, clear_some, 0)

        def fill_some(b, carry):
            for u in range(ISSUE_UNROLL):
                a = b * ISSUE_UNROLL + u
                tok_ref[dest_ref[a]] = a // TOP_K
            return carry
        lax.fori_loop(0, dest_ref.shape[0] // ISSUE_UNROLL, fill_some, 0)

    @pl.when(s == 0)
    def _():
        def start_some(b, carry):
            for u in range(ISSUE_UNROLL):
                r = b * ISSUE_UNROLL + u
                row_copy(tok_ref[start + r], r).start()
            return carry
        lax.fori_loop(0, n_chunks * (ROW_CHUNK // ISSUE_UNROLL), start_some, 0)

        def wait_chunk(c, carry):
            pltpu.make_async_copy(h_hbm.at[pl.ds(0, ROW_CHUNK), :], xu.at[rows_of(c), :], in_sem).wait()
            return carry
        lax.fori_loop(0, n_chunks, wait_chunk, 0)

        def cast_one(c, carry):
            xg[rows_of(c), :] = xu[rows_of(c), :].astype(BF16)
            return carry
        lax.fori_loop(0, n_chunks, cast_one, 0)

    @pl.when(jnp.logical_and(n_chunks > 0, s < nfu))
    def _():
        bup = bup_ref[0]
        col = pl.multiple_of(s * EXPERT_TF, EXPERT_TF)

        def up_step(first_chunk, m):
            r = pl.ds(pl.multiple_of(first_chunk * ROW_CHUNK, ROW_CHUNK), m)
            hup = jnp.dot(xg[r, :], wup_ref[0].astype(BF16), preferred_element_type=F32) + bup
            parts = []
            for t in range(2 * EXPERT_TF // LANES):
                hs = hup[:, t * LANES:(t + 1) * LANES]
                lin = jnp.clip(pltpu.roll(hs, LANES - 1, 1), -SWIGLU_LIMIT, SWIGLU_LIMIT)
                glu = jnp.minimum(hs, SWIGLU_LIMIT)
                parts.append((glu * jax.nn.sigmoid(SWIGLU_ALPHA * glu) * (lin + 1.0)).astype(BF16))
            act2 = jnp.concatenate(parts, axis=1)
            act[r, pl.ds(col, EXPERT_TF)] = jnp.dot(act2, sel_ref[...], preferred_element_type=F32).astype(BF16)
        _row_steps(n_chunks, up_step)

    @pl.when(jnp.logical_and(n_chunks > 0, s >= nfu))
    def _():
        bd = bd_ref[0]
        slot = (s - nfu) % 2
        drain(slot)

        def down_step(first_chunk, m):
            r = pl.ds(pl.multiple_of(first_chunk * ROW_CHUNK, ROW_CHUNK), m)
            ybuf[slot, r, :] = jnp.dot(act[r, :], wd_ref[0].astype(BF16), preferred_element_type=F32) + bd
        _row_steps(n_chunks, down_step)

        def start_one(c, carry):
            out_copy(c, slot, start, (s - nfu) * EXPERT_TN).start()
            return carry
        lax.fori_loop(0, n_chunks, start_one, 0)
        pend[slot] = n_chunks

    @pl.when(jnp.logical_and(g == pl.num_programs(0) - 1, s == pl.num_programs(1) - 1))
    def _():
        drain(0)
        drain(1)
        total_chunks = ys_hbm.shape[0] // ROW_CHUNK
        tail_chunks = total_chunks - ng_ref[1]
        ybuf[0, pl.ds(0, ROW_CHUNK), :] = jnp.zeros((ROW_CHUNK, EXPERT_TN), F32)
        for j in range(D_MODEL // EXPERT_TN):
            def start_one(c, carry, j=j):
                out_copy(0, 0, (ng_ref[1] + c) * ROW_CHUNK, j * EXPERT_TN).start()
                return carry
            lax.fori_loop(0, tail_chunks, start_one, 0)
        pend[0] = tail_chunks * (D_MODEL // EXPERT_TN)
        drain(0)


def _experts(h2, dest_flat, p, ge, gs, gn, ng, w_up, b_up, w_down, b_down, n_groups):
    tf, tn = EXPERT_TF, EXPERT_TN
    nfu, nfd = D_EXPERT // tf, D_MODEL // tn
    sel = np.zeros((2 * tf, tf), np.float32)
    sel[2 * np.arange(tf), np.arange(tf)] = 1.0
    sel = jnp.asarray(sel, BF16)

    def up_i(g, s, ng):
        return jnp.minimum(s, nfu - 1)

    def dn_i(g, s, ng):
        return jnp.clip(s - nfu, 0, nfd - 1)

    grid_spec = pltpu.PrefetchScalarGridSpec(
        num_scalar_prefetch=5,
        grid=(ng[0], nfu + nfd),
        in_specs=[pl.BlockSpec(memory_space=pl.ANY),
                  pl.BlockSpec((1, D_MODEL, 2 * tf), lambda g, s, ge, gs, gn, ng, tk: (ge[g], 0, up_i(g, s, ng))),
                  pl.BlockSpec((1, 1, 2 * tf), lambda g, s, ge, gs, gn, ng, tk: (ge[g], 0, up_i(g, s, ng))),
                  pl.BlockSpec((1, D_EXPERT, tn), lambda g, s, ge, gs, gn, ng, tk: (ge[g], 0, dn_i(g, s, ng))),
                  pl.BlockSpec((1, 1, tn), lambda g, s, ge, gs, gn, ng, tk: (ge[g], 0, dn_i(g, s, ng))),
                  pl.BlockSpec((2 * tf, tf), lambda g, s, ge, gs, gn, ng, tk: (0, 0))],
        out_specs=pl.BlockSpec(memory_space=pl.ANY),
        scratch_shapes=[pltpu.VMEM((GROUP_CAP, D_MODEL), F32),
                        pltpu.VMEM((GROUP_CAP, D_MODEL), BF16),
                        pltpu.VMEM((GROUP_CAP, D_EXPERT), BF16),
                        pltpu.VMEM((2, GROUP_CAP, tn), F32),
                        pltpu.SMEM((p,), jnp.int32),
                        pltpu.SMEM((2,), jnp.int32),
                        pltpu.SemaphoreType.DMA(()),
                        pltpu.SemaphoreType.DMA((2,))],
    )
    return pl.pallas_call(
        _expert_kernel,
        grid_spec=grid_spec,
        out_shape=jax.ShapeDtypeStruct((p, D_MODEL), F32),
        compiler_params=_params("arbitrary", "arbitrary"),
        name="experts",
    )(ge, gs, gn, ng, dest_flat, h2, w_up, b_up, w_down, b_down, sel)


FINAL_TM = 256


def _final_kernel(dest_ref, x1_ref, w_ref, g2_ref, fg_ref, ys_hbm, o_ref, buf, sem, *, tm):
    i = pl.program_id(0)
    n = pl.num_programs(0)

    def row_copy(row, slot, k, r):
        return pltpu.make_async_copy(ys_hbm.at[pl.ds(row, 1), :], buf.at[slot, k, pl.ds(r, 1), :], sem.at[slot])

    def issue(step, slot):
        per_iter = ISSUE_UNROLL // TOP_K

        def some(b, carry):
            for u in range(per_iter):
                r = b * per_iter + u
                for k in range(TOP_K):
                    row_copy(dest_ref[(step * tm + r) * TOP_K + k], slot, k, r).start()
            return carry
        lax.fori_loop(0, tm // per_iter, some, 0)

    @pl.when(i == 0)
    def _():
        issue(0, 0)

    @pl.when(i + 1 < n)
    def _():
        issue(i + 1, (i + 1) % 2)

    slot = i % 2

    for k in range(TOP_K):
        pltpu.make_async_copy(ys_hbm.at[pl.ds(0, tm), :], buf.at[slot, k], sem.at[slot]).wait()

    w = w_ref[...]
    moe = w[:, 0:1] * buf[slot, 0]
    for k in range(1, TOP_K):
        moe = moe + w[:, k:k + 1] * buf[slot, k]
    x2 = x1_ref[...] + g2_ref[...] * moe
    ms = jnp.mean(x2 * x2, axis=-1, keepdims=True)
    o_ref[...] = x2 * lax.rsqrt(ms + NORM_EPS) * fg_ref[...]


def _final(dest_flat, x1, wts_t, g2, final_g, ys):
    s = x1.shape[0]
    tm = min(FINAL_TM, s)
    kern = functools.partial(_final_kernel, tm=tm)
    grid_spec = pltpu.PrefetchScalarGridSpec(
        num_scalar_prefetch=1,
        grid=(s // tm,),
        in_specs=[pl.BlockSpec((tm, D_MODEL), lambda i, d: (i, 0)),
                  pl.BlockSpec((tm, TOP_K), lambda i, d: (i, 0)),
                  pl.BlockSpec((1, D_MODEL), lambda i, d: (0, 0)),
                  pl.BlockSpec((1, D_MODEL), lambda i, d: (0, 0)),
                  pl.BlockSpec(memory_space=pl.ANY)],
        out_specs=pl.BlockSpec((tm, D_MODEL), lambda i, d: (i, 0)),
        scratch_shapes=[pltpu.VMEM((2, TOP_K, tm, D_MODEL), F32),
                        pltpu.SemaphoreType.DMA((2,))],
    )
    return pl.pallas_call(
        kern,
        grid_spec=grid_spec,
        out_shape=jax.ShapeDtypeStruct((s, D_MODEL), F32),
        compiler_params=_params("arbitrary"),
        name="final_norm",
    )(dest_flat, x1, wts_t, g2, final_g, ys)


def _lam_kernel(q1_ref, k1_ref, q2_ref, k2_ref, o_ref):
    a = jnp.sum(q1_ref[...] * k1_ref[...], axis=-1, keepdims=True)
    b = jnp.sum(q2_ref[...] * k2_ref[...], axis=-1, keepdims=True)
    o_ref[...] = jnp.exp(a) - jnp.exp(b) + LAMBDA_INIT


def _lambda(lq1, lk1, lq2, lk2):
    return pl.pallas_call(
        _lam_kernel,
        out_shape=jax.ShapeDtypeStruct((1, 1), F32),
        name="diff_lambda",
    )(lq1, lk1, lq2, lk2)


def _group_tables(padded, pstart, n_groups):
    per_e = (padded + GROUP_CAP - 1) // GROUP_CAP
    gend = jnp.cumsum(per_e)
    gidx = jnp.arange(n_groups, dtype=jnp.int32)
    e_of = jnp.minimum(jnp.sum((gend[None, :] <= gidx[:, None]).astype(jnp.int32), axis=1), N_EXPERTS - 1)
    k_of = gidx - (gend[e_of] - per_e[e_of])
    used = gidx < gend[-1]
    rows = jnp.clip(padded[e_of] - k_of * GROUP_CAP, 0, GROUP_CAP)
    last_e = jnp.max(jnp.where(per_e > 0, jnp.arange(N_EXPERTS, dtype=jnp.int32), 0))
    ge = jnp.where(used, e_of, last_e).astype(jnp.int32)
    gs = jnp.where(used, pstart[e_of] + k_of * GROUP_CAP, 0).astype(jnp.int32)
    gn = jnp.where(used, rows // ROW_CHUNK, 0).astype(jnp.int32)
    used_chunks = (pstart[-1] + padded[-1]) // ROW_CHUNK
    ng = jnp.stack([gend[-1], used_chunks]).astype(jnp.int32)
    return ge, gs, gn, ng


def kernel(x, c, positions, w_ada, b_ada, norm1_g, w_in, conv_w, lambda_q1, lambda_k1, lambda_q2, lambda_k2, subln_g, w_attn_out, w_conv_out, w_o, norm2_g, w_router, b_router, w_up, b_up, w_down, b_down, final_g):
    b, s, d = x.shape
    assert b == 1 and d == D_MODEL
    t = s
    x2d = x.reshape(t, d)

    mod = _modulation(c, w_ada[0], b_ada[0])
    sh1, sc1, g1, sh2, sc2, g2 = [mod[:, k * d:(k + 1) * d] for k in range(N_MOD)]

    inv_freq = ROPE_THETA ** (-jnp.arange(0, ROPE_DIM, 2, dtype=F32) / ROPE_DIM)
    invf = jnp.concatenate([inv_freq, inv_freq, jnp.zeros((HEAD_DIM - ROPE_DIM,), F32)]).reshape(1, HEAD_DIM)
    pos_col = positions.reshape(t, 1)

    proj = _in_projection(x2d, norm1_g, sc1, sh1, pos_col, invf, w_in[0])
    lam = _lambda(lambda_q1, lambda_k1, lambda_q2, lambda_k2)
    o_attn = _diff_attention(proj, lam, subln_g)
    merged = _branch_merge(o_attn, proj, conv_w[0], w_attn_out[0].astype(BF16), w_conv_out[0].astype(BF16))
    x1, h2, logits_t = _out_projection(merged, x2d, g1, w_o[0].astype(BF16), norm2_g, sc2, sh2,
                                        w_router[0].T, b_router[0].reshape(N_EXPERTS, 1))

    dest, wts, cnt = _route(logits_t)
    a = t * TOP_K
    p = -(-(a + N_EXPERTS * (ROW_CHUNK - 1)) // ROW_CHUNK) * ROW_CHUNK
    n_groups = N_EXPERTS + -(-p // GROUP_CAP)
    ge, gs, gn, ng = _group_tables(cnt[:, 1], cnt[:, 2], n_groups)
    dest_flat = dest.T.reshape(-1)

    ys = _experts(h2, dest_flat, p, ge, gs, gn, ng, w_up[0], b_up[0].reshape(N_EXPERTS, 1, 2 * D_EXPERT),
                  w_down[0], b_down[0].reshape(N_EXPERTS, 1, d), n_groups)

    out = _final(dest_flat, x1, wts.T, g2, final_g.reshape(1, d), ys)
    return out.reshape(b, s, d)
```
